```python
import math
import jax, jax.numpy as jnp
from jax import lax
import numpy as np

D_MODEL = 4096
BATCH = 2
SEQ = 4096
DEPTH = 1
DEC_BATCH = 1
DEC_SEQ = 16384
PAST_LEN = 128

N_HEADS = 16
HEAD_DIM = 64
V_HEAD_DIM = 2 * HEAD_DIM
QK_WIDTH = N_HEADS * 2 * HEAD_DIM
ATTN_WIDTH = N_HEADS * V_HEAD_DIM
ROT_DIM = HEAD_DIM // 4
ROPE_THETA = 500000.0
Q_BLOCK = 128
RNN_WIDTH = 2048
RNN_BLOCKS = 16
RNN_BLOCK_W = RNN_WIDTH // RNN_BLOCKS
CONV_WIDTH = 4
RG_C = 8.0
N_BRANCH = 2
IN_WIDTH = 2 * QK_WIDTH + ATTN_WIDTH + 2 * RNN_WIDTH + N_BRANCH * D_MODEL
D_FF = 11008
EPS = 1e-6

kernel_name = "hybrid_diffattn_rglru_macaron_encoder"


def _lambda_init(layer_idx):
    return 0.8 - 0.6 * math.exp(-0.3 * layer_idx)


def _rms_norm(x, g):
    xf = x.astype(jnp.float32)
    y = xf * lax.rsqrt(jnp.mean(xf * xf, axis=-1, keepdims=True) + EPS)
    return (y * g.astype(jnp.float32)).astype(x.dtype)


def _swiglu(x, w_in, w_out):
    gate, up = jnp.split(x @ w_in, 2, axis=-1)
    return (jax.nn.silu(gate) * up) @ w_out


def _partial_rope(x, pos):
    inv_freq = ROPE_THETA ** (-jnp.arange(0, ROT_DIM, 2, dtype=jnp.float32) / ROT_DIM)
    ang = pos.astype(jnp.float32)[:, None] * inv_freq[None, :]
    cos = jnp.cos(ang)[:, None, None, :].astype(x.dtype)
    sin = jnp.sin(ang)[:, None, None, :].astype(x.dtype)
    xr, xp = x[..., :ROT_DIM], x[..., ROT_DIM:]
    x1, x2 = jnp.split(xr, 2, axis=-1)
    rot = jnp.concatenate([x1 * cos - x2 * sin, x2 * cos + x1 * sin], axis=-1)
    return jnp.concatenate([rot, xp], axis=-1)


def _diff_attention(q, k, v, lam):
    B, S = q.shape[0], q.shape[1]
    nb = S // Q_BLOCK
    q = q * (HEAD_DIM ** -0.5)
    qb = q.reshape(B, nb, Q_BLOCK, N_HEADS, 2, HEAD_DIM).swapaxes(0, 1)

    def block(qi):
        s = jnp.einsum('bqhmd,bkhmd->bhmqk', qi, k).astype(jnp.float32)
        p = jax.nn.softmax(s, axis=-1)
        w = p[:, :, 0] - lam * p[:, :, 1]
        return jnp.einsum('bhqk,bkhe->bqhe', w.astype(v.dtype), v)

    o = lax.map(block, qb)
    return o.swapaxes(0, 1).reshape(B, S, N_HEADS, V_HEAD_DIM)


def _centred_dwconv(x, w, b):
    C = x.shape[-1]
    y = lax.conv_general_dilated(x, w[:, None, :].astype(x.dtype), window_strides=(1,),
                                 padding=[(CONV_WIDTH // 2, CONV_WIDTH - 1 - CONV_WIDTH // 2)],
                                 dimension_numbers=('NWC', 'WIO', 'NWC'),
                                 feature_group_count=C)
    return y + b.astype(x.dtype)


def _rglru_dir(xf, wa, ba, wx, bx, lam, reverse):
    B, S, _ = xf.shape
    xb = xf.reshape(B, S, RNN_BLOCKS, RNN_BLOCK_W)
    r = jax.nn.sigmoid(jnp.einsum('bsnc,ncd->bsnd', xb, wa.astype(jnp.float32)) + ba.astype(jnp.float32)).reshape(B, S, RNN_WIDTH)
    i = jax.nn.sigmoid(jnp.einsum('bsnc,ncd->bsnd', xb, wx.astype(jnp.float32)) + bx.astype(jnp.float32)).reshape(B, S, RNN_WIDTH)
    log_a = -RG_C * r * jax.nn.softplus(-lam.astype(jnp.float32))
    a = jnp.exp(log_a)
    u = jnp.sqrt(-jnp.expm1(2.0 * log_a)) * (i * xf)

    def combine(e1, e2):
        a1, b1 = e1
        a2, b2 = e2
        return a1 * a2, a2 * b1 + b2

    _, h = lax.associative_scan(combine, (a, u), axis=1, reverse=reverse)
    return h


def _layer(x, l, ffn1_norm, ffn1_w_in, ffn1_w_out, mix_norm, w_in, lambda_q1, lambda_k1,
           lambda_q2, lambda_k2, subln_gain, conv_w, conv_b, rg_wa, rg_ba, rg_wx, rg_bx,
           rg_lambda, w_attn_proj, w_rnn_proj, w_out, ffn2_norm, ffn2_w_in, ffn2_w_out):
    B, S, _ = x.shape
    lam_init = _lambda_init(l)
    x = x + 0.5 * _swiglu(_rms_norm(x, ffn1_norm[l]), ffn1_w_in[l], ffn1_w_out[l])

    h = _rms_norm(x, mix_norm[l])
    proj = h @ w_in[l]
    c1 = QK_WIDTH
    c2 = c1 + QK_WIDTH
    c3 = c2 + ATTN_WIDTH
    c4 = c3 + RNN_WIDTH
    c5 = c4 + RNN_WIDTH
    q, k, v, rx, ry, gates = jnp.split(proj, [c1, c2, c3, c4, c5], axis=-1)

    pos = jnp.arange(S)
    q = _partial_rope(q.reshape(B, S, N_HEADS, 2, HEAD_DIM), pos)
    k = _partial_rope(k.reshape(B, S, N_HEADS, 2, HEAD_DIM), pos)
    v = v.reshape(B, S, N_HEADS, V_HEAD_DIM)
    f32 = jnp.float32
    lam = (jnp.exp(jnp.sum(lambda_q1[l].astype(f32) * lambda_k1[l].astype(f32)))
           - jnp.exp(jnp.sum(lambda_q2[l].astype(f32) * lambda_k2[l].astype(f32))) + lam_init)
    o = _diff_attention(q, k, v, lam)
    o = _rms_norm(o, subln_gain[l]) * (1.0 - lam_init)
    branch_a = o.reshape(B, S, ATTN_WIDTH) @ w_attn_proj[l]

    xc = _centred_dwconv(rx, conv_w[l], conv_b[l]).astype(f32)
    hf = _rglru_dir(xc, rg_wa[l, 0], rg_ba[l, 0], rg_wx[l, 0], rg_bx[l, 0], rg_lambda[l, 0], False)
    hb = _rglru_dir(xc, rg_wa[l, 1], rg_ba[l, 1], rg_wx[l, 1], rg_bx[l, 1], rg_lambda[l, 1], True)
    rr = ((hf + hb) * jax.nn.gelu(ry.astype(f32))).astype(x.dtype)
    branch_r = rr @ w_rnn_proj[l]

    g = jax.nn.sigmoid(gates.reshape(B, S, N_BRANCH, D_MODEL))
    merged = g[:, :, 0] * branch_a + g[:, :, 1] * branch_r
    x = x + merged @ w_out[l]

    x = x + 0.5 * _swiglu(_rms_norm(x, ffn2_norm[l]), ffn2_w_in[l], ffn2_w_out[l])
    return x


def setup_inputs(seed: int = 0) -> dict:
    key = jax.random.key(seed)
    ks = jax.random.split(key, 32)
    f32 = jnp.float32

    def nrm(k, shape, fan_in):
        return jax.random.normal(k, shape, f32) * (fan_in ** -0.5)

    def gain(k, shape):
        return 1.0 + 0.01 * jax.random.normal(k, shape, f32)

    a0 = jax.random.uniform(ks[20], (DEPTH, 2, RNN_WIDTH), f32, 0.9, 0.999)
    return {
        "x_prompt": jax.random.normal(ks[0], (BATCH, SEQ, D_MODEL), f32),
        "x_sample": jax.random.normal(ks[1], (DEC_BATCH, DEC_SEQ, D_MODEL), f32),
        "ffn1_norm": gain(ks[2], (DEPTH, D_MODEL)),
        "ffn1_w_in": nrm(ks[3], (DEPTH, D_MODEL, 2 * D_FF), D_MODEL),
        "ffn1_w_out": nrm(ks[4], (DEPTH, D_FF, D_MODEL), D_FF),
        "mix_norm": gain(ks[5], (DEPTH, D_MODEL)),
        "w_in": nrm(ks[6], (DEPTH, D_MODEL, IN_WIDTH), D_MODEL),
        "lambda_q1": 0.1 * jax.random.normal(ks[7], (DEPTH, HEAD_DIM), f32),
        "lambda_k1": 0.1 * jax.random.normal(ks[8], (DEPTH, HEAD_DIM), f32),
        "lambda_q2": 0.1 * jax.random.normal(ks[9], (DEPTH, HEAD_DIM), f32),
        "lambda_k2": 0.1 * jax.random.normal(ks[10], (DEPTH, HEAD_DIM), f32),
        "subln_gain": gain(ks[11], (DEPTH, V_HEAD_DIM)),
        "conv_w": nrm(ks[12], (DEPTH, CONV_WIDTH, RNN_WIDTH), CONV_WIDTH),
        "conv_b": 0.01 * jax.random.normal(ks[13], (DEPTH, RNN_WIDTH), f32),
        "rg_wa": nrm(ks[14], (DEPTH, 2, RNN_BLOCKS, RNN_BLOCK_W, RNN_BLOCK_W), RNN_BLOCK_W),
        "rg_ba": 0.01 * jax.random.normal(ks[15], (DEPTH, 2, RNN_BLOCKS, RNN_BLOCK_W), f32),
        "rg_wx": nrm(ks[16], (DEPTH, 2, RNN_BLOCKS, RNN_BLOCK_W, RNN_BLOCK_W), RNN_BLOCK_W),
        "rg_bx": 0.01 * jax.random.normal(ks[17], (DEPTH, 2, RNN_BLOCKS, RNN_BLOCK_W), f32),
        "rg_lambda": jnp.log(a0) - jnp.log1p(-a0),
        "w_attn_proj": nrm(ks[18], (DEPTH, ATTN_WIDTH, D_MODEL), ATTN_WIDTH),
        "w_rnn_proj": nrm(ks[19], (DEPTH, RNN_WIDTH, D_MODEL), RNN_WIDTH),
        "w_out": nrm(ks[21], (DEPTH, D_MODEL, D_MODEL), D_MODEL),
        "ffn2_norm": gain(ks[22], (DEPTH, D_MODEL)),
        "ffn2_w_in": nrm(ks[23], (DEPTH, D_MODEL, 2 * D_FF), D_MODEL),
        "ffn2_w_out": nrm(ks[24], (DEPTH, D_FF, D_MODEL), D_FF),
        "final_norm": gain(ks[25], (D_MODEL,)),
    }


def _trunk(x, ffn1_norm, ffn1_w_in, ffn1_w_out, mix_norm, w_in, lambda_q1, lambda_k1,
           lambda_q2, lambda_k2, subln_gain, conv_w, conv_b, rg_wa, rg_ba, rg_wx, rg_bx,
           rg_lambda, w_attn_proj, w_rnn_proj, w_out, ffn2_norm, ffn2_w_in, ffn2_w_out, final_norm):
    for l in range(DEPTH):
        x = _layer(x, l, ffn1_norm, ffn1_w_in, ffn1_w_out, mix_norm, w_in, lambda_q1, lambda_k1,
                   lambda_q2, lambda_k2, subln_gain, conv_w, conv_b, rg_wa, rg_ba, rg_wx, rg_bx,
                   rg_lambda, w_attn_proj, w_rnn_proj, w_out, ffn2_norm, ffn2_w_in, ffn2_w_out)
    return _rms_norm(x, final_norm)


def reference(x_prompt, x_sample, ffn1_norm, ffn1_w_in, ffn1_w_out, mix_norm, w_in, lambda_q1,
              lambda_k1, lambda_q2, lambda_k2, subln_gain, conv_w, conv_b, rg_wa, rg_ba, rg_wx,
              rg_bx, rg_lambda, w_attn_proj, w_rnn_proj, w_out, ffn2_norm, ffn2_w_in, ffn2_w_out,
              final_norm):
    y_prompt = _trunk(x_prompt, ffn1_norm, ffn1_w_in, ffn1_w_out, mix_norm, w_in, lambda_q1,
                      lambda_k1, lambda_q2, lambda_k2, subln_gain, conv_w, conv_b, rg_wa, rg_ba,
                      rg_wx, rg_bx, rg_lambda, w_attn_proj, w_rnn_proj, w_out, ffn2_norm,
                      ffn2_w_in, ffn2_w_out, final_norm)
    y_sample = _trunk(x_sample, ffn1_norm, ffn1_w_in, ffn1_w_out, mix_norm, w_in, lambda_q1,
                      lambda_k1, lambda_q2, lambda_k2, subln_gain, conv_w, conv_b, rg_wa, rg_ba,
                      rg_wx, rg_bx, rg_lambda, w_attn_proj, w_rnn_proj, w_out, ffn2_norm,
                      ffn2_w_in, ffn2_w_out, final_norm)
    return (y_prompt, y_sample)
```

```python
import functools
import math

import jax
import jax.numpy as jnp
from jax import lax
from jax.experimental import pallas as pl
from jax.experimental.pallas import tpu as pltpu

EPS = 1e-6
ROPE_THETA = 500000.0
RG_C = 8.0
LAMBDA_INIT = 0.8 - 0.6 * math.exp(-0.3 * 0)

V7X_LANES = 128
V7X_SUBLANES = 8
V7X_VMEM_BYTES = 64 * 1024 * 1024
VMEM_CAP = V7X_VMEM_BYTES - 6 * 1024 * 1024

F32 = jnp.float32
BF16 = jnp.bfloat16


def _pick(n, target, align):
    best = None
    for d in range(align, min(n, target) + 1, align):
        if n % d == 0:
            best = d
    return best if best is not None else n


def _nbytes(shape, dtype):
    return math.prod(shape) * jnp.dtype(dtype).itemsize


def _params(semantics, pipelined_bytes, resident_bytes=0):
    need = 2 * pipelined_bytes + resident_bytes + 4 * 1024 * 1024
    return pltpu.CompilerParams(dimension_semantics=semantics,
                                vmem_limit_bytes=int(min(max(need, 16 * 1024 * 1024), VMEM_CAP)))


def _rmsnorm_kernel(x_ref, g_ref, o_ref):
    x = x_ref[...]
    ms = jnp.mean(x * x, axis=-1, keepdims=True)
    o_ref[...] = (x * lax.rsqrt(ms + EPS) * g_ref[...]).astype(o_ref.dtype)


def _rmsnorm(x, g, out_dtype):
    M, D = x.shape
    bm = _pick(M, 256, V7X_SUBLANES)
    blocks = _nbytes((bm, D), F32) + _nbytes((bm, D), out_dtype)
    return pl.pallas_call(
        _rmsnorm_kernel,
        grid=(M // bm,),
        in_specs=[pl.BlockSpec((bm, D), lambda i: (i, 0)),
                  pl.BlockSpec((1, D), lambda i: (0, 0))],
        out_specs=pl.BlockSpec((bm, D), lambda i: (i, 0)),
        out_shape=jax.ShapeDtypeStruct((M, D), out_dtype),
        compiler_params=_params(("parallel",), blocks, _nbytes((bm, D), F32)),
        name="rmsnorm",
    )(x, g.reshape(1, D).astype(F32))


def _ffn_up_kernel(h_ref, wg_ref, wu_ref, o_ref):
    h = h_ref[...]
    g = jnp.dot(h, wg_ref[...], preferred_element_type=F32)
    u = jnp.dot(h, wu_ref[...], preferred_element_type=F32)
    o_ref[...] = (g * jax.nn.sigmoid(g) * u).astype(o_ref.dtype)


def _ffn_up(h, wg, wu):
    M, D = h.shape
    Fp = wg.shape[1]
    bm = _pick(M, 1024, V7X_SUBLANES)
    bf = _pick(Fp, 512, V7X_LANES)
    blocks = _nbytes((bm, D), BF16) + 2 * _nbytes((D, bf), BF16) + _nbytes((bm, bf), BF16)
    return pl.pallas_call(
        _ffn_up_kernel,
        grid=(M // bm, Fp // bf),
        in_specs=[pl.BlockSpec((bm, D), lambda i, j: (i, 0)),
                  pl.BlockSpec((D, bf), lambda i, j: (0, j)),
                  pl.BlockSpec((D, bf), lambda i, j: (0, j))],
        out_specs=pl.BlockSpec((bm, bf), lambda i, j: (i, j)),
        out_shape=jax.ShapeDtypeStruct((M, Fp), BF16),
        compiler_params=_params(("parallel", "arbitrary"), blocks, 3 * _nbytes((bm, bf), F32)),
        name="ffn_up",
    )(h, wg, wu)


def _mm_res_kernel(a_ref, w_ref, r_ref, o_ref, acc_ref, *, scale):
    k = pl.program_id(2)

    @pl.when(k == 0)
    def _():
        acc_ref[...] = jnp.zeros_like(acc_ref)

    acc_ref[...] += jnp.dot(a_ref[...], w_ref[...], preferred_element_type=F32)

    @pl.when(k == pl.num_programs(2) - 1)
    def _():
        o_ref[...] = r_ref[...] + scale * acc_ref[...]


def _mm_residual(a, w, res, scale):
    M, K = a.shape
    N = w.shape[1]
    bm = _pick(M, 1024, V7X_SUBLANES)
    bn = _pick(N, 1024, V7X_LANES)
    bk = _pick(K, 2048, V7X_LANES)
    blocks = (_nbytes((bm, bk), BF16) + _nbytes((bk, bn), BF16) + 2 * _nbytes((bm, bn), F32))
    return pl.pallas_call(
        functools.partial(_mm_res_kernel, scale=scale),
        grid=(M // bm, N // bn, K // bk),
        in_specs=[pl.BlockSpec((bm, bk), lambda i, j, k: (i, k)),
                  pl.BlockSpec((bk, bn), lambda i, j, k: (k, j)),
                  pl.BlockSpec((bm, bn), lambda i, j, k: (i, j))],
        out_specs=pl.BlockSpec((bm, bn), lambda i, j, k: (i, j)),
        out_shape=jax.ShapeDtypeStruct((M, N), F32),
        scratch_shapes=[pltpu.VMEM((bm, bn), F32)],
        compiler_params=_params(("parallel", "parallel", "arbitrary"), blocks,
                                2 * _nbytes((bm, bn), F32)),
        name="mm_residual",
    )(a, w, res)


def _ffn(x, norm_g, wg, wu, wo):
    h = _rmsnorm(x, norm_g, BF16)
    a = _ffn_up(h, wg, wu)
    return _mm_residual(a, wo, x, 0.5)


def _mm_kernel(x_ref, w_ref, o_ref):
    o_ref[...] = jnp.dot(x_ref[...], w_ref[...], preferred_element_type=F32).astype(o_ref.dtype)


def _mm(x, w, out_dtype):
    M, K = x.shape
    N = w.shape[1]
    bm = _pick(M, 1024, V7X_SUBLANES)
    bn = _pick(N, 1024, V7X_LANES)
    blocks = _nbytes((bm, K), BF16) + _nbytes((K, bn), BF16) + _nbytes((bm, bn), out_dtype)
    return pl.pallas_call(
        _mm_kernel,
        grid=(M // bm, N // bn),
        in_specs=[pl.BlockSpec((bm, K), lambda i, j: (i, 0)),
                  pl.BlockSpec((K, bn), lambda i, j: (0, j))],
        out_specs=pl.BlockSpec((bm, bn), lambda i, j: (i, j)),
        out_shape=jax.ShapeDtypeStruct((M, N), out_dtype),
        compiler_params=_params(("parallel", "arbitrary"), blocks, _nbytes((bm, bn), F32)),
        name="mm",
    )(x, w)


def _nt_dot(w, h):
    return lax.dot_general(w, h, (((1,), (1,)), ((), ())), preferred_element_type=F32)


def _proj_vt_kernel(w_ref, h_ref, o_ref):
    o_ref[...] = _nt_dot(w_ref[...], h_ref[...]).astype(o_ref.dtype)


def _proj_qt_kernel(w_ref, h_ref, cos_ref, sin_ref, o_ref, *, head_dim, rot_half, scale):
    acc = _nt_dot(w_ref[...], h_ref[...])
    cos = cos_ref[...]
    sin = sin_ref[...]
    for g in range(acc.shape[0] // head_dim):
        base = g * head_dim
        x1 = acc[base:base + rot_half]
        x2 = acc[base + rot_half:base + 2 * rot_half]
        rest = acc[base + 2 * rot_half:base + head_dim]
        rot = jnp.concatenate([x1 * cos - x2 * sin, x2 * cos + x1 * sin, rest], axis=0)
        o_ref[base:base + head_dim, :] = (rot * scale).astype(o_ref.dtype)


def _proj_t(h3, wt, rope=None):
    B, S, K = h3.shape
    N = wt.shape[0]
    bm = _pick(S, 1024, V7X_LANES)
    bn = _pick(N, 1024, V7X_LANES)
    blocks = _nbytes((bm, K), BF16) + _nbytes((bn, K), BF16) + _nbytes((bn, bm), BF16)
    in_specs = [pl.BlockSpec((bn, K), lambda b, i, j: (j, 0)),
                pl.BlockSpec((None, bm, K), lambda b, i, j: (b, i, 0))]
    args = [wt, h3]
    if rope is None:
        body = _proj_vt_kernel
    else:
        cos_t, sin_t, head_dim, scale = rope
        rot_half = cos_t.shape[0]
        body = functools.partial(_proj_qt_kernel, head_dim=head_dim, rot_half=rot_half, scale=scale)
        in_specs += [pl.BlockSpec((rot_half, bm), lambda b, i, j: (0, i)),
                     pl.BlockSpec((rot_half, bm), lambda b, i, j: (0, i))]
        args += [cos_t, sin_t]
    return pl.pallas_call(
        body,
        grid=(B, S // bm, N // bn),
        in_specs=in_specs,
        out_specs=pl.BlockSpec((None, bn, bm), lambda b, i, j: (b, j, i)),
        out_shape=jax.ShapeDtypeStruct((B, N, S), BF16),
        compiler_params=_params(("parallel", "parallel", "arbitrary"), blocks,
                                2 * _nbytes((bn, bm), F32)),
        name="proj_t",
    )(*args)


def _proj_k_kernel(h_ref, w_ref, c_ref, s1_ref, s2_ref, o_ref, *, rot_half):
    acc = jnp.dot(h_ref[...], w_ref[...], preferred_element_type=F32)
    c = c_ref[...]
    s1 = s1_ref[...]
    s2 = s2_ref[...]
    for t in range(acc.shape[1] // V7X_LANES):
        x = acc[:, t * V7X_LANES:(t + 1) * V7X_LANES]
        up = pltpu.roll(x, V7X_LANES - rot_half, 1)
        dn = pltpu.roll(x, rot_half, 1)
        o_ref[:, t * V7X_LANES:(t + 1) * V7X_LANES] = (x * c + up * s1 + dn * s2).astype(o_ref.dtype)


def _proj_k(h3, w, c, s1, s2, rot_half):
    B, S, K = h3.shape
    N = w.shape[1]
    bm = _pick(S, 1024, V7X_SUBLANES)
    bn = _pick(N, 1024, V7X_LANES)
    blocks = (_nbytes((bm, K), BF16) + _nbytes((K, bn), BF16) + _nbytes((bm, bn), BF16)
              + 3 * _nbytes((bm, V7X_LANES), F32))
    tab = pl.BlockSpec((bm, V7X_LANES), lambda b, i, j: (i, 0))
    return pl.pallas_call(
        functools.partial(_proj_k_kernel, rot_half=rot_half),
        grid=(B, S // bm, N // bn),
        in_specs=[pl.BlockSpec((None, bm, K), lambda b, i, j: (b, i, 0)),
                  pl.BlockSpec((K, bn), lambda b, i, j: (0, j)),
                  tab, tab, tab],
        out_specs=pl.BlockSpec((None, bm, bn), lambda b, i, j: (b, i, j)),
        out_shape=jax.ShapeDtypeStruct((B, S, N), BF16),
        compiler_params=_params(("parallel", "parallel", "arbitrary"), blocks,
                                2 * _nbytes((bm, bn), F32)),
        name="proj_k",
    )(h3, w, c, s1, s2)


def _rope_tables(S, head_dim):
    rot_dim = head_dim // 4
    rot_half = rot_dim // 2
    inv_freq = ROPE_THETA ** (-jnp.arange(0, rot_dim, 2, dtype=F32) / rot_dim)
    ang = jnp.arange(S).astype(F32)[:, None] * inv_freq[None, :]
    cos, sin = jnp.cos(ang), jnp.sin(ang)
    d = jnp.arange(V7X_LANES) % head_dim
    idx = d % rot_half
    lo = d < rot_half
    hi = (d >= rot_half) & (d < rot_dim)
    c = jnp.where((lo | hi)[None, :], cos[:, idx], 1.0)
    s1 = jnp.where(lo[None, :], -sin[:, idx], 0.0)
    s2 = jnp.where(hi[None, :], sin[:, idx], 0.0)
    return cos.T, sin.T, c, s1, s2


def _attn_kernel(lam_ref, qt_ref, k_ref, vt_ref, gain_ref, o_ref, qbd_ref, acc1_ref, acc2_ref,
                 *, tk, head_dim):
    tq = qt_ref.shape[1]
    S = k_ref.shape[0]
    qbd_ref[...] = jnp.zeros_like(qbd_ref)
    qbd_ref[0:head_dim, 0:tq] = qt_ref[0:head_dim, :]
    qbd_ref[head_dim:2 * head_dim, tq:2 * tq] = qt_ref[head_dim:2 * head_dim, :]
    acc1_ref[...] = jnp.zeros_like(acc1_ref)
    acc2_ref[...] = jnp.zeros_like(acc2_ref)

    def update(s, m, l, acc_ref, v):
        m_new = jnp.maximum(m, jnp.max(s, axis=0, keepdims=True))
        alpha = jnp.exp(m - m_new)
        p = jnp.exp(s - m_new)
        l_new = alpha * l + jnp.sum(p, axis=0, keepdims=True)
        acc_ref[...] = alpha * acc_ref[...] + jnp.dot(v, p.astype(BF16), preferred_element_type=F32)
        return m_new, l_new

    def body(c, carry):
        m1, l1, m2, l2 = carry
        start = pl.multiple_of(c * tk, tk)
        s = jnp.dot(k_ref[pl.ds(start, tk), :], qbd_ref[...], preferred_element_type=F32)
        v = vt_ref[:, pl.ds(start, tk)]
        m1, l1 = update(s[:, :tq], m1, l1, acc1_ref, v)
        m2, l2 = update(s[:, tq:], m2, l2, acc2_ref, v)
        return m1, l1, m2, l2

    neg = jnp.full((1, tq), -jnp.inf, F32)
    zero = jnp.zeros((1, tq), F32)
    m1, l1, m2, l2 = lax.fori_loop(0, S // tk, body, (neg, zero, neg, zero))

    lq1, lk1, lq2, lk2 = (lam_ref[i:i + 1, :] for i in range(4))
    lam = (jnp.exp(jnp.sum(lq1 * lk1, axis=-1, keepdims=True))
           - jnp.exp(jnp.sum(lq2 * lk2, axis=-1, keepdims=True)) + LAMBDA_INIT)
    o = acc1_ref[...] / l1 - lam * (acc2_ref[...] / l2)
    ms = jnp.mean(o * o, axis=0, keepdims=True)
    o = o * lax.rsqrt(ms + EPS) * gain_ref[...] * (1.0 - LAMBDA_INIT)
    o_ref[...] = o.T.astype(o_ref.dtype)


def _diff_attention(qt, k, vt, lam_params, gain, head_dim):
    B, W, S = qt.shape
    hw = 2 * head_dim
    H = W // hw
    tq = _pick(S, 256, V7X_LANES)
    tk = _pick(S, 512, V7X_LANES)
    blocks = (_nbytes((hw, tq), BF16) + 2 * _nbytes((S, hw), BF16) + _nbytes((tq, hw), BF16))
    scratch = _nbytes((hw, 2 * tq), BF16) + 2 * _nbytes((hw, tq), F32)
    temps = 4 * _nbytes((tk, 2 * tq), F32)
    return pl.pallas_call(
        functools.partial(_attn_kernel, tk=tk, head_dim=head_dim),
        grid=(B, H, S // tq),
        in_specs=[pl.BlockSpec((4, head_dim), lambda b, h, i: (0, 0)),
                  pl.BlockSpec((None, hw, tq), lambda b, h, i: (b, h, i)),
                  pl.BlockSpec((None, S, hw), lambda b, h, i: (b, 0, h)),
                  pl.BlockSpec((None, hw, S), lambda b, h, i: (b, h, 0)),
                  pl.BlockSpec((hw, 1), lambda b, h, i: (0, 0))],
        out_specs=pl.BlockSpec((None, tq, hw), lambda b, h, i: (b, i, h)),
        out_shape=jax.ShapeDtypeStruct((B, S, W), BF16),
        scratch_shapes=[pltpu.VMEM((hw, 2 * tq), BF16),
                        pltpu.VMEM((hw, tq), F32),
                        pltpu.VMEM((hw, tq), F32)],
        compiler_params=_params(("parallel", "parallel", "arbitrary"), blocks, scratch + temps),
        name="diff_attention",
    )(lam_params, qt, k, vt, gain.reshape(hw, 1).astype(F32))


def _softplus(x):
    return jnp.maximum(x, 0.0) + jnp.log1p(jnp.exp(-jnp.abs(x)))


def _rglru_kernel(*refs, reverse, n_chunks, conv_width):
    if reverse:
        (x_ref, xp_ref, xn_ref, cw_ref, cb_ref, w_ref, b_ref, lam_ref, hf_ref, y_ref,
         o_ref, xe_ref, a_ref, u_ref, h_ref) = refs
    else:
        (x_ref, xp_ref, xn_ref, cw_ref, cb_ref, w_ref, b_ref, lam_ref,
         o_ref, xe_ref, a_ref, u_ref, h_ref) = refs
    T = x_ref.shape[0]
    halo = xp_ref.shape[0]
    nblk, bw = w_ref.shape[0], w_ref.shape[1]
    step = pl.program_id(1)
    chunk = (n_chunks - 1 - step) if reverse else step

    xe_ref[0:halo, :] = jnp.where(chunk == 0, 0.0, xp_ref[...])
    xe_ref[halo:halo + T, :] = x_ref[...]
    xe_ref[halo + T:2 * halo + T, :] = jnp.where(chunk == n_chunks - 1, 0.0, xn_ref[...])
    left = conv_width // 2
    xc = cb_ref[...] + cw_ref[0:1, :] * xe_ref[halo - left:halo - left + T, :]
    for j in range(1, conv_width):
        xc = xc + cw_ref[j:j + 1, :] * xe_ref[halo - left + j:halo - left + j + T, :]

    sp = _softplus(-lam_ref[...])
    for n in range(nblk):
        cols = slice(n * bw, (n + 1) * bw)
        xb = xc[:, cols]
        z = jnp.dot(xb.astype(BF16), w_ref[n], preferred_element_type=F32) + b_ref[n]
        r = jax.nn.sigmoid(z[:, :bw])
        i = jax.nn.sigmoid(z[:, bw:])
        log_a = -RG_C * r * sp[:, cols]
        a_ref[:, cols] = jnp.exp(log_a)
        gain = jnp.sqrt(jnp.maximum(1.0 - jnp.exp(2.0 * log_a), 0.0))
        u_ref[:, cols] = gain * (i * xb)

    @pl.when(step == 0)
    def _():
        h_ref[...] = jnp.zeros_like(h_ref)

    def group(g, h):
        gi = (T // V7X_SUBLANES - 1 - g) if reverse else g
        base = pl.multiple_of(gi * V7X_SUBLANES, V7X_SUBLANES)
        order = range(V7X_SUBLANES - 1, -1, -1) if reverse else range(V7X_SUBLANES)
        for r_ in order:
            row = pl.ds(base + r_, 1)
            h = a_ref[row, :] * h + u_ref[row, :]
            u_ref[row, :] = h
        return h

    h_ref[...] = lax.fori_loop(0, T // V7X_SUBLANES, group, h_ref[...])

    if reverse:
        o_ref[...] = ((hf_ref[...] + u_ref[...]) * jax.nn.gelu(y_ref[...])).astype(o_ref.dtype)
    else:
        o_ref[...] = u_ref[...]


def _rglru_pass(rest, x_col, y_col, conv_w, conv_b, wcat, bcat, lam, hf, reverse):
    B, S, _ = rest.shape
    nblk, bw, _ = wcat.shape
    C = nblk * bw
    conv_width = conv_w.shape[0]
    T = _pick(S, 256, V7X_SUBLANES)
    halo = V7X_SUBLANES
    n_chunks = S // T
    hpc = T // halo

    def cidx(c):
        return (n_chunks - 1 - c) if reverse else c

    in_specs = [
        pl.BlockSpec((None, T, C), lambda b, c: (b, cidx(c), x_col)),
        pl.BlockSpec((None, halo, C),
                     lambda b, c: (b, jnp.maximum(cidx(c) * hpc - 1, 0), x_col)),
        pl.BlockSpec((None, halo, C),
                     lambda b, c: (b, jnp.minimum((cidx(c) + 1) * hpc, S // halo - 1), x_col)),
        pl.BlockSpec((conv_width, C), lambda b, c: (0, 0)),
        pl.BlockSpec((1, C), lambda b, c: (0, 0)),
        pl.BlockSpec((nblk, bw, 2 * bw), lambda b, c: (0, 0, 0)),
        pl.BlockSpec((nblk, 1, 2 * bw), lambda b, c: (0, 0, 0)),
        pl.BlockSpec((1, C), lambda b, c: (0, 0)),
    ]
    args = [rest, rest, rest, conv_w, conv_b, wcat, bcat, lam]
    blocks = 2 * _nbytes((T, C), F32) + 2 * _nbytes((halo, C), F32) + 2 * _nbytes(wcat.shape, BF16)
    if reverse:
        in_specs += [pl.BlockSpec((None, T, C), lambda b, c: (b, cidx(c), 0)),
                     pl.BlockSpec((None, T, C), lambda b, c: (b, cidx(c), y_col))]
        args += [hf, rest]
        blocks += 2 * _nbytes((T, C), F32)
        out_dtype = BF16
    else:
        out_dtype = F32
    scratch = [pltpu.VMEM((T + 2 * halo, C), F32), pltpu.VMEM((T, C), F32),
               pltpu.VMEM((T, C), F32), pltpu.VMEM((1, C), F32)]
    return pl.pallas_call(
        functools.partial(_rglru_kernel, reverse=reverse, n_chunks=n_chunks, conv_width=conv_width),
        grid=(B, n_chunks),
        in_specs=in_specs,
        out_specs=pl.BlockSpec((None, T, C), lambda b, c: (b, cidx(c), 0)),
        out_shape=jax.ShapeDtypeStruct((B, S, C), out_dtype),
        scratch_shapes=scratch,
        compiler_params=_params(("parallel", "arbitrary"), blocks, 8 * _nbytes((T, C), F32)),
        name="rglru_bwd" if reverse else "rglru_fwd",
    )(*args)


def _merge_kernel(o_ref, r_ref, wa_ref, wr_ref, ga_ref, gr_ref, out_ref):
    a = jnp.dot(o_ref[...], wa_ref[...], preferred_element_type=F32)
    r = jnp.dot(r_ref[...], wr_ref[...], preferred_element_type=F32)
    out_ref[...] = (jax.nn.sigmoid(ga_ref[...]) * a + jax.nn.sigmoid(gr_ref[...]) * r).astype(out_ref.dtype)


def _merge(o, rr, wa, wr, rest, gate_col0):
    M, K = o.shape
    N = wa.shape[1]
    bm = _pick(M, 1024, V7X_SUBLANES)
    bn = _pick(N, 512, V7X_LANES)
    nb = N // bn
    ga0 = gate_col0 // bn
    blocks = (2 * _nbytes((bm, K), BF16) + 2 * _nbytes((K, bn), BF16) + 2 * _nbytes((bm, bn), F32)
              + _nbytes((bm, bn), BF16))
    return pl.pallas_call(
        _merge_kernel,
        grid=(M // bm, nb),
        in_specs=[pl.BlockSpec((bm, K), lambda i, j: (i, 0)),
                  pl.BlockSpec((bm, K), lambda i, j: (i, 0)),
                  pl.BlockSpec((K, bn), lambda i, j: (0, j)),
                  pl.BlockSpec((K, bn), lambda i, j: (0, j)),
                  pl.BlockSpec((bm, bn), lambda i, j: (i, ga0 + j)),
                  pl.BlockSpec((bm, bn), lambda i, j: (i, ga0 + nb + j))],
        out_specs=pl.BlockSpec((bm, bn), lambda i, j: (i, j)),
        out_shape=jax.ShapeDtypeStruct((M, N), BF16),
        compiler_params=_params(("parallel", "arbitrary"), blocks, 3 * _nbytes((bm, bn), F32)),
        name="merge",
    )(o, rr, wa, wr, rest, rest)


def _prepare_weights(ffn1_w_in, ffn1_w_out, w_in, conv_w, conv_b, rg_wa, rg_ba, rg_wx, rg_bx, rg_lambda,
                     w_attn_proj, w_rnn_proj, w_out, ffn2_w_in, ffn2_w_out, qk_width, rnn_width):
    def ffn_w(w_i, w_o):
        F = w_o.shape[0]
        Fp = -(-F // 1024) * 1024
        pad = Fp - F
        wg = jnp.pad(w_i[:, :F].astype(BF16), ((0, 0), (0, pad)))
        wu = jnp.pad(w_i[:, F:].astype(BF16), ((0, 0), (0, pad)))
        wo = jnp.pad(w_o.astype(BF16), ((0, pad), (0, 0)))
        return wg, wu, wo

    c1, c2, c3 = qk_width, 2 * qk_width, 3 * qk_width
    wi = w_in.astype(BF16)
    nblk, bw = rg_wa.shape[1], rg_wa.shape[2]
    return dict(
        ffn1=ffn_w(ffn1_w_in[0], ffn1_w_out[0]),
        ffn2=ffn_w(ffn2_w_in[0], ffn2_w_out[0]),
        wq_t=wi[:, :c1].T, wk=wi[:, c1:c2], wv_t=wi[:, c2:c3].T, w_rest=wi[:, c3:],
        conv_w=conv_w.astype(F32), conv_b=conv_b.reshape(1, rnn_width).astype(F32),
        wcat=jnp.concatenate([rg_wa, rg_wx], axis=-1).astype(BF16),
        bcat=jnp.concatenate([rg_ba, rg_bx], axis=-1).reshape(2, nblk, 1, 2 * bw).astype(F32),
        lam=rg_lambda.reshape(2, 1, rnn_width).astype(F32),
        w_attn=w_attn_proj.astype(BF16), w_rnn=w_rnn_proj.astype(BF16), w_out=w_out.astype(BF16),
    )


def _trunk(x, p, ffn1_norm, mix_norm, lam_params, subln_gain, ffn2_norm, final_norm, head_dim, rnn_width):
    B, S, D = x.shape
    M = B * S
    x = x.reshape(M, D)
    x = _ffn(x, ffn1_norm, *p["ffn1"])

    h = _rmsnorm(x, mix_norm, BF16)
    h3 = h.reshape(B, S, D)
    cos_t, sin_t, c, s1, s2 = _rope_tables(S, head_dim)
    qt = _proj_t(h3, p["wq_t"], rope=(cos_t, sin_t, head_dim, head_dim ** -0.5))
    k = _proj_k(h3, p["wk"], c, s1, s2, cos_t.shape[0])
    vt = _proj_t(h3, p["wv_t"])
    rest = _mm(h, p["w_rest"], F32)

    o = _diff_attention(qt, k, vt, lam_params, subln_gain, head_dim)

    rest3 = rest.reshape(B, S, rest.shape[1])
    rnn = (rest3, 0, 1, p["conv_w"], p["conv_b"])
    hf = _rglru_pass(*rnn, p["wcat"][0], p["bcat"][0], p["lam"][0], None, reverse=False)
    rr = _rglru_pass(*rnn, p["wcat"][1], p["bcat"][1], p["lam"][1], hf, reverse=True)

    merged = _merge(o.reshape(M, -1), rr.reshape(M, rnn_width), p["w_attn"], p["w_rnn"], rest, 2 * rnn_width)
    x = _mm_residual(merged, p["w_out"], x, 1.0)

    x = _ffn(x, ffn2_norm, *p["ffn2"])
    return _rmsnorm(x, final_norm, F32).reshape(B, S, D)


def kernel(x_prompt, x_sample, ffn1_norm, ffn1_w_in, ffn1_w_out, mix_norm, w_in, lambda_q1, lambda_k1, lambda_q2, lambda_k2, subln_gain, conv_w, conv_b, rg_wa, rg_ba, rg_wx, rg_bx, rg_lambda, w_attn_proj, w_rnn_proj, w_out, ffn2_norm, ffn2_w_in, ffn2_w_out, final_norm):
    head_dim = lambda_q1.shape[-1]
    rnn_width = conv_w.shape[-1]
    D = x_prompt.shape[-1]
    qk_width = (w_in.shape[-1] - 2 * rnn_width - 2 * D) // 3
    p = _prepare_weights(ffn1_w_in, ffn1_w_out, w_in[0], conv_w[0], conv_b[0], rg_wa[0], rg_ba[0],
                         rg_wx[0], rg_bx[0], rg_lambda[0], w_attn_proj[0], w_rnn_proj[0], w_out[0],
                         ffn2_w_in, ffn2_w_out, qk_width, rnn_width)
    lam_params = jnp.concatenate([lambda_q1, lambda_k1, lambda_q2, lambda_k2], axis=0).astype(F32)
    args = (p, ffn1_norm[0], mix_norm[0], lam_params, subln_gain[0], ffn2_norm[0], final_norm,
            head_dim, rnn_width)
    return _trunk(x_prompt, *args), _trunk(x_sample, *args)
```

```python
import functools
import math

import jax
import jax.numpy as jnp
from jax import lax
from jax.experimental import pallas as pl
from jax.experimental.pallas import tpu as pltpu

EPS = 1e-6
ROPE_THETA = 500000.0
RG_C = 8.0
LAMBDA_INIT = 0.8 - 0.6 * math.exp(-0.3 * 0)

V7X_LANES = 128
V7X_SUBLANES = 8
V7X_VMEM_BYTES = 64 * 1024 * 1024
VMEM_CAP = V7X_VMEM_BYTES - 6 * 1024 * 1024

F32 = jnp.float32
BF16 = jnp.bfloat16


def _pick(n, target, align):
    best = None
    for d in range(align, min(n, target) + 1, align):
        if n % d == 0:
            best = d
    return best if best is not None else n


def _nbytes(shape, dtype):
    return math.prod(shape) * jnp.dtype(dtype).itemsize


def _params(semantics, pipelined_bytes, resident_bytes=0):
    need = 2 * pipelined_bytes + resident_bytes + 4 * 1024 * 1024
    return pltpu.CompilerParams(dimension_semantics=semantics,
                                vmem_limit_bytes=int(min(max(need, 16 * 1024 * 1024), VMEM_CAP)))


def _rmsnorm_kernel(x_ref, g_ref, o_ref):
    x = x_ref[...]
    ms = jnp.mean(x * x, axis=-1, keepdims=True)
    o_ref[...] = (x * lax.rsqrt(ms + EPS) * g_ref[...]).astype(o_ref.dtype)


def _rmsnorm(x, g, out_dtype):
    M, D = x.shape
    bm = _pick(M, 256, V7X_SUBLANES)
    blocks = _nbytes((bm, D), F32) + _nbytes((bm, D), out_dtype)
    return pl.pallas_call(
        _rmsnorm_kernel,
        grid=(M // bm,),
        in_specs=[pl.BlockSpec((bm, D), lambda i: (i, 0)),
                  pl.BlockSpec((1, D), lambda i: (0, 0))],
        out_specs=pl.BlockSpec((bm, D), lambda i: (i, 0)),
        out_shape=jax.ShapeDtypeStruct((M, D), out_dtype),
        compiler_params=_params(("parallel",), blocks, _nbytes((bm, D), F32)),
        name="rmsnorm",
    )(x, g.reshape(1, D).astype(F32))


def _ffn_up_kernel(h_ref, wg_ref, wu_ref, o_ref):
    h = h_ref[...]
    g = jnp.dot(h, wg_ref[...], preferred_element_type=F32)
    u = jnp.dot(h, wu_ref[...], preferred_element_type=F32)
    o_ref[...] = (g * jax.nn.sigmoid(g) * u).astype(o_ref.dtype)


def _ffn_up(h, w_gu):
    M, D = h.shape
    Fp = w_gu.shape[1] // 2
    bm = _pick(M, 1024, V7X_SUBLANES)
    bf = _pick(Fp, 512, V7X_LANES)
    nf = Fp // bf
    blocks = _nbytes((bm, D), BF16) + 2 * _nbytes((D, bf), BF16) + _nbytes((bm, bf), BF16)
    return pl.pallas_call(
        _ffn_up_kernel,
        grid=(M // bm, nf),
        in_specs=[pl.BlockSpec((bm, D), lambda i, j: (i, 0)),
                  pl.BlockSpec((D, bf), lambda i, j: (0, j)),
                  pl.BlockSpec((D, bf), lambda i, j: (0, nf + j))],
        out_specs=pl.BlockSpec((bm, bf), lambda i, j: (i, j)),
        out_shape=jax.ShapeDtypeStruct((M, Fp), BF16),
        compiler_params=_params(("parallel", "arbitrary"), blocks, 3 * _nbytes((bm, bf), F32)),
        name="ffn_up",
    )(h, w_gu, w_gu)


def _mm_res_kernel(a_ref, w_ref, r_ref, o_ref, *, scale):
    part = scale * jnp.dot(a_ref[...], w_ref[...], preferred_element_type=F32)

    @pl.when(pl.program_id(2) == 0)
    def _():
        o_ref[...] = r_ref[...] + part

    @pl.when(pl.program_id(2) != 0)
    def _():
        o_ref[...] += part


def _mm_residual(a, w, res, scale):
    M, K = a.shape
    N = w.shape[1]
    bm = _pick(M, 1024, V7X_SUBLANES)
    bn = _pick(N, 1024, V7X_LANES)
    bk = _pick(K, 3072, 2 * V7X_LANES)
    blocks = (_nbytes((bm, bk), BF16) + _nbytes((bk, bn), BF16) + 2 * _nbytes((bm, bn), F32))
    return pl.pallas_call(
        functools.partial(_mm_res_kernel, scale=scale),
        grid=(M // bm, N // bn, K // bk),
        in_specs=[pl.BlockSpec((bm, bk), lambda i, j, k: (i, k)),
                  pl.BlockSpec((bk, bn), lambda i, j, k: (k, j)),
                  pl.BlockSpec((bm, bn), lambda i, j, k: (i, j))],
        out_specs=pl.BlockSpec((bm, bn), lambda i, j, k: (i, j)),
        out_shape=jax.ShapeDtypeStruct((M, N), F32),
        compiler_params=_params(("parallel", "parallel", "arbitrary"), blocks,
                                2 * _nbytes((bm, bn), F32)),
        name="mm_residual",
    )(a, w, res)


def _ffn(x, norm_g, w_gu, wo):
    h = _rmsnorm(x, norm_g, BF16)
    a = _ffn_up(h, w_gu)
    return _mm_residual(a, wo, x, 0.5)


def _mm_kernel(x_ref, w_ref, o_ref):
    o_ref[...] = jnp.dot(x_ref[...], w_ref[...], preferred_element_type=F32).astype(o_ref.dtype)


def _mm(x, w, col0, N, out_dtype):
    M, K = x.shape
    bm = _pick(M, 1024, V7X_SUBLANES)
    bn = _pick(math.gcd(N, col0) if col0 else N, 1024, V7X_LANES)
    j0 = col0 // bn
    blocks = _nbytes((bm, K), BF16) + _nbytes((K, bn), BF16) + _nbytes((bm, bn), out_dtype)
    return pl.pallas_call(
        _mm_kernel,
        grid=(M // bm, N // bn),
        in_specs=[pl.BlockSpec((bm, K), lambda i, j: (i, 0)),
                  pl.BlockSpec((K, bn), lambda i, j: (0, j0 + j))],
        out_specs=pl.BlockSpec((bm, bn), lambda i, j: (i, j)),
        out_shape=jax.ShapeDtypeStruct((M, N), out_dtype),
        compiler_params=_params(("parallel", "arbitrary"), blocks, _nbytes((bm, bn), F32)),
        name="mm",
    )(x, w)


def _nt_dot(w, h):
    return lax.dot_general(w, h, (((1,), (1,)), ((), ())), preferred_element_type=F32)


def _proj_vt_kernel(w_ref, h_ref, o_ref):
    o_ref[...] = _nt_dot(w_ref[...], h_ref[...]).astype(o_ref.dtype)


def _proj_qt_kernel(w_ref, h_ref, cos_ref, sin_ref, o_ref, *, head_dim, rot_half, scale):
    acc = _nt_dot(w_ref[...], h_ref[...])
    cos = cos_ref[...]
    sin = sin_ref[...]
    for g in range(acc.shape[0] // head_dim):
        base = g * head_dim
        x1 = acc[base:base + rot_half]
        x2 = acc[base + rot_half:base + 2 * rot_half]
        rest = acc[base + 2 * rot_half:base + head_dim]
        rot = jnp.concatenate([x1 * cos - x2 * sin, x2 * cos + x1 * sin, rest], axis=0)
        o_ref[base:base + head_dim, :] = (rot * scale).astype(o_ref.dtype)


def _proj_t(h3, wt, rope=None):
    B, S, K = h3.shape
    N = wt.shape[0]
    bm = _pick(S, 1024, V7X_LANES)
    bn = _pick(N, 1024, V7X_LANES)
    blocks = _nbytes((bm, K), BF16) + _nbytes((bn, K), BF16) + _nbytes((bn, bm), BF16)
    in_specs = [pl.BlockSpec((bn, K), lambda b, i, j: (j, 0)),
                pl.BlockSpec((None, bm, K), lambda b, i, j: (b, i, 0))]
    args = [wt, h3]
    if rope is None:
        body = _proj_vt_kernel
    else:
        cos_t, sin_t, head_dim, scale = rope
        rot_half = cos_t.shape[0]
        body = functools.partial(_proj_qt_kernel, head_dim=head_dim, rot_half=rot_half, scale=scale)
        in_specs += [pl.BlockSpec((rot_half, bm), lambda b, i, j: (0, i)),
                     pl.BlockSpec((rot_half, bm), lambda b, i, j: (0, i))]
        args += [cos_t, sin_t]
    return pl.pallas_call(
        body,
        grid=(B, S // bm, N // bn),
        in_specs=in_specs,
        out_specs=pl.BlockSpec((None, bn, bm), lambda b, i, j: (b, j, i)),
        out_shape=jax.ShapeDtypeStruct((B, N, S), BF16),
        compiler_params=_params(("parallel", "parallel", "arbitrary"), blocks,
                                2 * _nbytes((bn, bm), F32)),
        name="proj_t",
    )(*args)


def _proj_k_kernel(h_ref, w_ref, c_ref, s1_ref, s2_ref, o_ref, *, rot_half):
    acc = jnp.dot(h_ref[...], w_ref[...], preferred_element_type=F32)
    c = c_ref[...]
    s1 = s1_ref[...]
    s2 = s2_ref[...]
    for t in range(acc.shape[1] // V7X_LANES):
        x = acc[:, t * V7X_LANES:(t + 1) * V7X_LANES]
        up = pltpu.roll(x, V7X_LANES - rot_half, 1)
        dn = pltpu.roll(x, rot_half, 1)
        o_ref[:, t * V7X_LANES:(t + 1) * V7X_LANES] = (x * c + up * s1 + dn * s2).astype(o_ref.dtype)


def _proj_k(h3, w, col0, N, c, s1, s2, rot_half):
    B, S, K = h3.shape
    bm = _pick(S, 1024, V7X_SUBLANES)
    bn = _pick(math.gcd(N, col0) if col0 else N, 1024, V7X_LANES)
    j0 = col0 // bn
    blocks = (_nbytes((bm, K), BF16) + _nbytes((K, bn), BF16) + _nbytes((bm, bn), BF16)
              + 3 * _nbytes((bm, V7X_LANES), F32))
    tab = pl.BlockSpec((bm, V7X_LANES), lambda b, i, j: (i, 0))
    return pl.pallas_call(
        functools.partial(_proj_k_kernel, rot_half=rot_half),
        grid=(B, S // bm, N // bn),
        in_specs=[pl.BlockSpec((None, bm, K), lambda b, i, j: (b, i, 0)),
                  pl.BlockSpec((K, bn), lambda b, i, j: (0, j0 + j)),
                  tab, tab, tab],
        out_specs=pl.BlockSpec((None, bm, bn), lambda b, i, j: (b, i, j)),
        out_shape=jax.ShapeDtypeStruct((B, S, N), BF16),
        compiler_params=_params(("parallel", "parallel", "arbitrary"), blocks,
                                2 * _nbytes((bm, bn), F32)),
        name="proj_k",
    )(h3, w, c, s1, s2)


def _rope_tables(S, head_dim):
    rot_dim = head_dim // 4
    rot_half = rot_dim // 2
    inv_freq = ROPE_THETA ** (-jnp.arange(0, rot_dim, 2, dtype=F32) / rot_dim)
    ang = jnp.arange(S).astype(F32)[:, None] * inv_freq[None, :]
    cos, sin = jnp.cos(ang), jnp.sin(ang)
    d = jnp.arange(V7X_LANES) % head_dim
    idx = d % rot_half
    lo = d < rot_half
    hi = (d >= rot_half) & (d < rot_dim)
    c = jnp.where((lo | hi)[None, :], cos[:, idx], 1.0)
    s1 = jnp.where(lo[None, :], -sin[:, idx], 0.0)
    s2 = jnp.where(hi[None, :], sin[:, idx], 0.0)
    return cos.T, sin.T, c, s1, s2


def _attn_kernel(lam_ref, qt_ref, k_ref, vt_ref, gain_ref, o_ref,
                 qbd_ref, vte_ref, sa_ref, sb_ref, acc1_ref, acc2_ref, *, tk, head_dim):
    tq = qt_ref.shape[1]
    S = k_ref.shape[0]
    hw = 2 * head_dim
    n_chunks = S // tk

    @pl.when(pl.program_id(2) == 0)
    def _():
        vte_ref[0:hw, :] = vt_ref[...]
        vte_ref[hw:, :] = jnp.ones((vte_ref.shape[0] - hw, S), vte_ref.dtype)

    qbd_ref[...] = jnp.zeros_like(qbd_ref)
    qbd_ref[0:head_dim, 0:tq] = qt_ref[0:head_dim, :]
    qbd_ref[head_dim:hw, tq:2 * tq] = qt_ref[head_dim:hw, :]
    acc1_ref[...] = jnp.zeros_like(acc1_ref)
    acc2_ref[...] = jnp.zeros_like(acc2_ref)

    def scores(c, s_ref):
        start = pl.multiple_of(c * tk, tk)
        s_ref[...] = jnp.dot(k_ref[pl.ds(start, tk), :], qbd_ref[...], preferred_element_type=F32)

    def accumulate(c, s_ref, m1, m2):
        start = pl.multiple_of(c * tk, tk)
        v = vte_ref[:, pl.ds(start, tk)]
        ms = []
        for m, acc_ref, cols in ((m1, acc1_ref, slice(0, tq)), (m2, acc2_ref, slice(tq, 2 * tq))):
            s = s_ref[:, cols]
            m_new = jnp.maximum(m, jnp.max(s, axis=0, keepdims=True))
            alpha = jnp.exp2(m - m_new)
            p = jnp.exp2(s - m_new).astype(BF16)
            acc_ref[...] = alpha * acc_ref[...] + jnp.dot(v, p, preferred_element_type=F32)
            ms.append(m_new)
        return ms

    scores(0, sa_ref)

    per_trip = 4 if n_chunks % 4 == 0 else 2
    bufs = (sa_ref, sb_ref)

    def body(j, carry):
        m1, m2 = carry
        c = per_trip * j
        for i in range(per_trip):
            scores(jnp.minimum(c + i + 1, n_chunks - 1), bufs[(i + 1) % 2])
            m1, m2 = accumulate(c + i, bufs[i % 2], m1, m2)
        return m1, m2

    neg = jnp.full((1, tq), -jnp.inf, F32)
    lax.fori_loop(0, n_chunks // per_trip, body, (neg, neg))

    lq1, lk1, lq2, lk2 = (lam_ref[i:i + 1, :] for i in range(4))
    lam = (jnp.exp(jnp.sum(lq1 * lk1, axis=-1, keepdims=True))
           - jnp.exp(jnp.sum(lq2 * lk2, axis=-1, keepdims=True)) + LAMBDA_INIT)
    o = (acc1_ref[0:hw, :] / acc1_ref[hw:hw + 1, :]
         - lam * (acc2_ref[0:hw, :] / acc2_ref[hw:hw + 1, :]))
    ms = jnp.mean(o * o, axis=0, keepdims=True)
    o = o * lax.rsqrt(ms + EPS) * gain_ref[...] * (1.0 - LAMBDA_INIT)
    o_ref[...] = o.T.astype(o_ref.dtype)


ONES_ROWS = 2 * V7X_SUBLANES


def _diff_attention(qt, k, vt, lam_params, gain, head_dim):
    B, W, S = qt.shape
    hw = 2 * head_dim
    H = W // hw
    tq = _pick(S, 256, V7X_LANES)
    tk = _pick(S // 2, 512, V7X_LANES)
    blocks = (_nbytes((hw, tq), BF16) + 2 * _nbytes((S, hw), BF16) + _nbytes((tq, hw), BF16))
    scratch = (_nbytes((hw, 2 * tq), BF16) + _nbytes((hw + ONES_ROWS, S), BF16)
               + 2 * _nbytes((tk, 2 * tq), F32) + 2 * _nbytes((hw + ONES_ROWS, tq), F32))
    temps = 2 * _nbytes((tk, 2 * tq), F32)
    return pl.pallas_call(
        functools.partial(_attn_kernel, tk=tk, head_dim=head_dim),
        grid=(B, H, S // tq),
        in_specs=[pl.BlockSpec((4, head_dim), lambda b, h, i: (0, 0)),
                  pl.BlockSpec((None, hw, tq), lambda b, h, i: (b, h, i)),
                  pl.BlockSpec((None, S, hw), lambda b, h, i: (b, 0, h)),
                  pl.BlockSpec((None, hw, S), lambda b, h, i: (b, h, 0)),
                  pl.BlockSpec((hw, 1), lambda b, h, i: (0, 0))],
        out_specs=pl.BlockSpec((None, tq, hw), lambda b, h, i: (b, i, h)),
        out_shape=jax.ShapeDtypeStruct((B, S, W), BF16),
        scratch_shapes=[pltpu.VMEM((hw, 2 * tq), BF16),
                        pltpu.VMEM((hw + ONES_ROWS, S), BF16),
                        pltpu.VMEM((tk, 2 * tq), F32),
                        pltpu.VMEM((tk, 2 * tq), F32),
                        pltpu.VMEM((hw + ONES_ROWS, tq), F32),
                        pltpu.VMEM((hw + ONES_ROWS, tq), F32)],
        compiler_params=_params(("arbitrary", "arbitrary", "arbitrary"), blocks, scratch + temps),
        name="diff_attention",
    )(lam_params, qt, k, vt, gain.reshape(hw, 1).astype(F32))


def _softplus(x):
    return jnp.maximum(x, 0.0) + jnp.log1p(jnp.exp(-jnp.abs(x)))


def _rglru_kernel(*refs, reverse, n_chunks, conv_width):
    if reverse:
        (x_ref, xp_ref, xn_ref, cw_ref, cb_ref, w_ref, b_ref, lam_ref, hf_ref, y_ref,
         o_ref, xe_ref, a_ref, u_ref, h_ref) = refs
    else:
        (x_ref, xp_ref, xn_ref, cw_ref, cb_ref, w_ref, b_ref, lam_ref,
         o_ref, xe_ref, a_ref, u_ref, h_ref) = refs
    T = x_ref.shape[0]
    halo = xp_ref.shape[0]
    nblk, bw = w_ref.shape[0], w_ref.shape[1]
    step = pl.program_id(1)
    chunk = (n_chunks - 1 - step) if reverse else step

    xe_ref[0:halo, :] = jnp.where(chunk == 0, 0.0, xp_ref[...])
    xe_ref[halo:halo + T, :] = x_ref[...]
    xe_ref[halo + T:2 * halo + T, :] = jnp.where(chunk == n_chunks - 1, 0.0, xn_ref[...])
    left = conv_width // 2
    xc = cb_ref[...] + cw_ref[0:1, :] * xe_ref[halo - left:halo - left + T, :]
    for j in range(1, conv_width):
        xc = xc + cw_ref[j:j + 1, :] * xe_ref[halo - left + j:halo - left + j + T, :]

    sp = _softplus(-lam_ref[...])
    for n in range(nblk):
        cols = slice(n * bw, (n + 1) * bw)
        xb = xc[:, cols]
        z = jnp.dot(xb.astype(BF16), w_ref[n], preferred_element_type=F32) + b_ref[n]
        r = jax.nn.sigmoid(z[:, :bw])
        i = jax.nn.sigmoid(z[:, bw:])
        log_a = -RG_C * r * sp[:, cols]
        a_ref[:, cols] = jnp.exp(log_a)
        g = jnp.maximum(1.0 - jnp.exp(2.0 * log_a), 0.0)
        gain = g * lax.rsqrt(jnp.maximum(g, jnp.finfo(F32).tiny))
        u_ref[:, cols] = gain * (i * xb)

    @pl.when(step == 0)
    def _():
        h_ref[...] = jnp.zeros_like(h_ref)

    def group(g, h):
        gi = (T // V7X_SUBLANES - 1 - g) if reverse else g
        base = pl.multiple_of(gi * V7X_SUBLANES, V7X_SUBLANES)
        order = range(V7X_SUBLANES - 1, -1, -1) if reverse else range(V7X_SUBLANES)
        for r_ in order:
            row = pl.ds(base + r_, 1)
            h = a_ref[row, :] * h + u_ref[row, :]
            u_ref[row, :] = h
        return h

    h_ref[...] = lax.fori_loop(0, T // V7X_SUBLANES, group, h_ref[...])

    if reverse:
        o_ref[...] = ((hf_ref[...] + u_ref[...]) * jax.nn.gelu(y_ref[...])).astype(o_ref.dtype)
    else:
        o_ref[...] = u_ref[...]


def _rglru_pass(rest, x_col, y_col, conv_w, conv_b, wcat, bcat, lam, hf, reverse):
    B, S, _ = rest.shape
    nblk, bw, _ = wcat.shape
    C = nblk * bw
    conv_width = conv_w.shape[0]
    T = _pick(S, 256, V7X_SUBLANES)
    halo = V7X_SUBLANES
    n_chunks = S // T
    hpc = T // halo

    def cidx(c):
        return (n_chunks - 1 - c) if reverse else c

    in_specs = [
        pl.BlockSpec((None, T, C), lambda b, c: (b, cidx(c), x_col)),
        pl.BlockSpec((None, halo, C),
                     lambda b, c: (b, jnp.maximum(cidx(c) * hpc - 1, 0), x_col)),
        pl.BlockSpec((None, halo, C),
                     lambda b, c: (b, jnp.minimum((cidx(c) + 1) * hpc, S // halo - 1), x_col)),
        pl.BlockSpec((conv_width, C), lambda b, c: (0, 0)),
        pl.BlockSpec((1, C), lambda b, c: (0, 0)),
        pl.BlockSpec((nblk, bw, 2 * bw), lambda b, c: (0, 0, 0)),
        pl.BlockSpec((nblk, 1, 2 * bw), lambda b, c: (0, 0, 0)),
        pl.BlockSpec((1, C), lambda b, c: (0, 0)),
    ]
    args = [rest, rest, rest, conv_w, conv_b, wcat, bcat, lam]
    blocks = 2 * _nbytes((T, C), F32) + 2 * _nbytes((halo, C), F32) + 2 * _nbytes(wcat.shape, BF16)
    if reverse:
        in_specs += [pl.BlockSpec((None, T, C), lambda b, c: (b, cidx(c), 0)),
                     pl.BlockSpec((None, T, C), lambda b, c: (b, cidx(c), y_col))]
        args += [hf, rest]
        blocks += 2 * _nbytes((T, C), F32)
        out_dtype = BF16
    else:
        out_dtype = F32
    scratch = [pltpu.VMEM((T + 2 * halo, C), F32), pltpu.VMEM((T, C), F32),
               pltpu.VMEM((T, C), F32), pltpu.VMEM((1, C), F32)]
    return pl.pallas_call(
        functools.partial(_rglru_kernel, reverse=reverse, n_chunks=n_chunks, conv_width=conv_width),
        grid=(B, n_chunks),
        in_specs=in_specs,
        out_specs=pl.BlockSpec((None, T, C), lambda b, c: (b, cidx(c), 0)),
        out_shape=jax.ShapeDtypeStruct((B, S, C), out_dtype),
        scratch_shapes=scratch,
        compiler_params=_params(("parallel", "arbitrary"), blocks, 8 * _nbytes((T, C), F32)),
        name="rglru_bwd" if reverse else "rglru_fwd",
    )(*args)


def _merge_kernel(o_ref, r_ref, wa_ref, wr_ref, ga_ref, gr_ref, out_ref):
    a = jnp.dot(o_ref[...], wa_ref[...], preferred_element_type=F32)
    r = jnp.dot(r_ref[...], wr_ref[...], preferred_element_type=F32)
    out_ref[...] = (jax.nn.sigmoid(ga_ref[...]) * a + jax.nn.sigmoid(gr_ref[...]) * r).astype(out_ref.dtype)


def _merge(o, rr, wa, wr, rest, gate_col0):
    M, K = o.shape
    N = wa.shape[1]
    bm = _pick(M, 1024, V7X_SUBLANES)
    bn = _pick(N, 512, V7X_LANES)
    nb = N // bn
    ga0 = gate_col0 // bn
    blocks = (2 * _nbytes((bm, K), BF16) + 2 * _nbytes((K, bn), BF16) + 2 * _nbytes((bm, bn), F32)
              + _nbytes((bm, bn), BF16))
    return pl.pallas_call(
        _merge_kernel,
        grid=(M // bm, nb),
        in_specs=[pl.BlockSpec((bm, K), lambda i, j: (i, 0)),
                  pl.BlockSpec((bm, K), lambda i, j: (i, 0)),
                  pl.BlockSpec((K, bn), lambda i, j: (0, j)),
                  pl.BlockSpec((K, bn), lambda i, j: (0, j)),
                  pl.BlockSpec((bm, bn), lambda i, j: (i, ga0 + j)),
                  pl.BlockSpec((bm, bn), lambda i, j: (i, ga0 + nb + j))],
        out_specs=pl.BlockSpec((bm, bn), lambda i, j: (i, j)),
        out_shape=jax.ShapeDtypeStruct((M, N), BF16),
        compiler_params=_params(("parallel", "arbitrary"), blocks, 3 * _nbytes((bm, bn), F32)),
        name="merge",
    )(o, rr, wa, wr, rest, rest)


def _prepare_weights(ffn1_w_in, ffn1_w_out, w_in, conv_w, conv_b, rg_wa, rg_ba, rg_wx, rg_bx, rg_lambda,
                     w_attn_proj, w_rnn_proj, w_out, ffn2_w_in, ffn2_w_out, qk_width, rnn_width):
    def ffn_w(w_i, w_o):
        D, F = w_i.shape[0], w_o.shape[0]
        Fp = -(-F // 1024) * 1024
        pad = Fp - F
        w_gu = jnp.pad(w_i.astype(BF16).reshape(D, 2, F), ((0, 0), (0, 0), (0, pad))).reshape(D, 2 * Fp)
        wo = jnp.pad(w_o.astype(BF16), ((0, pad), (0, 0)))
        return w_gu, wo

    c1, c2, c3 = qk_width, 2 * qk_width, 3 * qk_width
    wi = w_in.astype(BF16)
    nblk, bw = rg_wa.shape[1], rg_wa.shape[2]
    return dict(
        ffn1=ffn_w(ffn1_w_in[0], ffn1_w_out[0]),
        ffn2=ffn_w(ffn2_w_in[0], ffn2_w_out[0]),
        w_in=wi, wq_t=wi[:, :c1].T, wv_t=wi[:, c2:c3].T,
        conv_w=conv_w.astype(F32), conv_b=conv_b.reshape(1, rnn_width).astype(F32),
        wcat=jnp.concatenate([rg_wa, rg_wx], axis=-1).astype(BF16),
        bcat=jnp.concatenate([rg_ba, rg_bx], axis=-1).reshape(2, nblk, 1, 2 * bw).astype(F32),
        lam=rg_lambda.reshape(2, 1, rnn_width).astype(F32),
        w_attn=w_attn_proj.astype(BF16), w_rnn=w_rnn_proj.astype(BF16), w_out=w_out.astype(BF16),
    )


def _trunk(x, p, ffn1_norm, mix_norm, lam_params, subln_gain, ffn2_norm, final_norm, head_dim, rnn_width):
    B, S, D = x.shape
    M = B * S
    x = x.reshape(M, D)
    x = _ffn(x, ffn1_norm, *p["ffn1"])

    h = _rmsnorm(x, mix_norm, BF16)
    h3 = h.reshape(B, S, D)
    cos_t, sin_t, c, s1, s2 = _rope_tables(S, head_dim)
    qt = _proj_t(h3, p["wq_t"], rope=(cos_t, sin_t, head_dim, head_dim ** -0.5 * math.log2(math.e)))
    qk_width = p["wq_t"].shape[0]
    k = _proj_k(h3, p["w_in"], qk_width, qk_width, c, s1, s2, cos_t.shape[0])
    vt = _proj_t(h3, p["wv_t"])
    c3 = 3 * qk_width
    rest = _mm(h, p["w_in"], c3, p["w_in"].shape[1] - c3, F32)

    o = _diff_attention(qt, k, vt, lam_params, subln_gain, head_dim)

    rest3 = rest.reshape(B, S, rest.shape[1])
    rnn = (rest3, 0, 1, p["conv_w"], p["conv_b"])
    hf = _rglru_pass(*rnn, p["wcat"][0], p["bcat"][0], p["lam"][0], None, reverse=False)
    rr = _rglru_pass(*rnn, p["wcat"][1], p["bcat"][1], p["lam"][1], hf, reverse=True)

    merged = _merge(o.reshape(M, -1), rr.reshape(M, rnn_width), p["w_attn"], p["w_rnn"], rest, 2 * rnn_width)
    x = _mm_residual(merged, p["w_out"], x, 1.0)

    x = _ffn(x, ffn2_norm, *p["ffn2"])
    return _rmsnorm(x, final_norm, F32).reshape(B, S, D)


def kernel(x_prompt, x_sample, ffn1_norm, ffn1_w_in, ffn1_w_out, mix_norm, w_in, lambda_q1, lambda_k1, lambda_q2, lambda_k2, subln_gain, conv_w, conv_b, rg_wa, rg_ba, rg_wx, rg_bx, rg_lambda, w_attn_proj, w_rnn_proj, w_out, ffn2_norm, ffn2_w_in, ffn2_w_out, final_norm):
    head_dim = lambda_q1.shape[-1]
    rnn_width = conv_w.shape[-1]
    D = x_prompt.shape[-1]
    qk_width = (w_in.shape[-1] - 2 * rnn_width - 2 * D) // 3
    p = _prepare_weights(ffn1_w_in, ffn1_w_out, w_in[0], conv_w[0], conv_b[0], rg_wa[0], rg_ba[0],
                         rg_wx[0], rg_bx[0], rg_lambda[0], w_attn_proj[0], w_rnn_proj[0], w_out[0],
                         ffn2_w_in, ffn2_w_out, qk_width, rnn_width)
    lam_params = jnp.concatenate([lambda_q1, lambda_k1, lambda_q2, lambda_k2], axis=0).astype(F32)
    args = (p, ffn1_norm[0], mix_norm[0], lam_params, subln_gain[0], ffn2_norm[0], final_norm,
            head_dim, rnn_width)
    return _trunk(x_prompt, *args), _trunk(x_sample, *args)
```

```python
import functools
import math

import jax
import jax.numpy as jnp
from jax import lax
from jax.experimental import pallas as pl
from jax.experimental.pallas import tpu as pltpu

EPS = 1e-6
ROPE_THETA = 500000.0
RG_C = 8.0
LAMBDA_INIT = 0.8 - 0.6 * math.exp(-0.3 * 0)

V7X_LANES = 128
V7X_SUBLANES = 8
V7X_VMEM_BYTES = 64 * 1024 * 1024
VMEM_CAP = V7X_VMEM_BYTES - 6 * 1024 * 1024

F32 = jnp.float32
BF16 = jnp.bfloat16


def _pick(n, target, align):
    best = None
    for d in range(align, min(n, target) + 1, align):
        if n % d == 0:
            best = d
    return best if best is not None else n


def _nbytes(shape, dtype):
    return math.prod(shape) * jnp.dtype(dtype).itemsize


def _params(semantics, pipelined_bytes, resident_bytes=0):
    need = 2 * pipelined_bytes + resident_bytes + 4 * 1024 * 1024
    return pltpu.CompilerParams(dimension_semantics=semantics,
                                vmem_limit_bytes=int(min(max(need, 16 * 1024 * 1024), VMEM_CAP)))


def _rmsnorm_kernel(x_ref, g_ref, o_ref):
    x = x_ref[...]
    ms = jnp.mean(x * x, axis=-1, keepdims=True)
    o_ref[...] = (x * lax.rsqrt(ms + EPS) * g_ref[...]).astype(o_ref.dtype)


def _rmsnorm(x, g, out_dtype):
    M, D = x.shape
    bm = _pick(M, 256, V7X_SUBLANES)
    blocks = _nbytes((bm, D), F32) + _nbytes((bm, D), out_dtype)
    return pl.pallas_call(
        _rmsnorm_kernel,
        grid=(M // bm,),
        in_specs=[pl.BlockSpec((bm, D), lambda i: (i, 0)),
                  pl.BlockSpec((1, D), lambda i: (0, 0))],
        out_specs=pl.BlockSpec((bm, D), lambda i: (i, 0)),
        out_shape=jax.ShapeDtypeStruct((M, D), out_dtype),
        compiler_params=_params(("parallel",), blocks, _nbytes((bm, D), F32)),
        name="rmsnorm",
    )(x, g.reshape(1, D).astype(F32))


def _ffn_up_kernel(h_ref, wg_ref, wu_ref, o_ref):
    h = h_ref[...]
    g = jnp.dot(h, wg_ref[...], preferred_element_type=F32)
    u = jnp.dot(h, wu_ref[...], preferred_element_type=F32)
    o_ref[...] = (g * jax.nn.sigmoid(g) * u).astype(o_ref.dtype)


def _ffn_up(h, w_gu):
    M, D = h.shape
    Fp = w_gu.shape[1] // 2
    bm = _pick(M, 1024, V7X_SUBLANES)
    bf = _pick(Fp, 512, V7X_LANES)
    nf = Fp // bf
    blocks = _nbytes((bm, D), BF16) + 2 * _nbytes((D, bf), BF16) + _nbytes((bm, bf), BF16)
    return pl.pallas_call(
        _ffn_up_kernel,
        grid=(M // bm, nf),
        in_specs=[pl.BlockSpec((bm, D), lambda i, j: (i, 0)),
                  pl.BlockSpec((D, bf), lambda i, j: (0, j)),
                  pl.BlockSpec((D, bf), lambda i, j: (0, nf + j))],
        out_specs=pl.BlockSpec((bm, bf), lambda i, j: (i, j)),
        out_shape=jax.ShapeDtypeStruct((M, Fp), BF16),
        compiler_params=_params(("parallel", "arbitrary"), blocks, 3 * _nbytes((bm, bf), F32)),
        name="ffn_up",
    )(h, w_gu, w_gu)


def _mm_res_kernel(a_ref, w_ref, r_ref, o_ref, *, scale):
    part = scale * jnp.dot(a_ref[...], w_ref[...], preferred_element_type=F32)

    @pl.when(pl.program_id(2) == 0)
    def _():
        o_ref[...] = r_ref[...] + part

    @pl.when(pl.program_id(2) != 0)
    def _():
        o_ref[...] += part


def _mm_residual(a, w, res, scale):
    M, K = a.shape
    N = w.shape[1]
    bm = _pick(M, 1024, V7X_SUBLANES)
    bn = _pick(N, 1024, V7X_LANES)
    bk = _pick(K, 3072, 2 * V7X_LANES)
    blocks = (_nbytes((bm, bk), BF16) + _nbytes((bk, bn), BF16) + 2 * _nbytes((bm, bn), F32))
    return pl.pallas_call(
        functools.partial(_mm_res_kernel, scale=scale),
        grid=(M // bm, N // bn, K // bk),
        in_specs=[pl.BlockSpec((bm, bk), lambda i, j, k: (i, k)),
                  pl.BlockSpec((bk, bn), lambda i, j, k: (k, j)),
                  pl.BlockSpec((bm, bn), lambda i, j, k: (i, j))],
        out_specs=pl.BlockSpec((bm, bn), lambda i, j, k: (i, j)),
        out_shape=jax.ShapeDtypeStruct((M, N), F32),
        compiler_params=_params(("parallel", "parallel", "arbitrary"), blocks,
                                2 * _nbytes((bm, bn), F32)),
        name="mm_residual",
    )(a, w, res)


def _ffn(x, norm_g, w_gu, wo):
    h = _rmsnorm(x, norm_g, BF16)
    a = _ffn_up(h, w_gu)
    return _mm_residual(a, wo, x, 0.5)


def _mm_kernel(x_ref, w_ref, o_ref):
    o_ref[...] = jnp.dot(x_ref[...], w_ref[...], preferred_element_type=F32).astype(o_ref.dtype)


def _mm(x, w, col0, N, out_dtype):
    M, K = x.shape
    bm = _pick(M, 1024, V7X_SUBLANES)
    bn = _pick(math.gcd(N, col0) if col0 else N, 1024, V7X_LANES)
    j0 = col0 // bn
    blocks = _nbytes((bm, K), BF16) + _nbytes((K, bn), BF16) + _nbytes((bm, bn), out_dtype)
    return pl.pallas_call(
        _mm_kernel,
        grid=(M // bm, N // bn),
        in_specs=[pl.BlockSpec((bm, K), lambda i, j: (i, 0)),
                  pl.BlockSpec((K, bn), lambda i, j: (0, j0 + j))],
        out_specs=pl.BlockSpec((bm, bn), lambda i, j: (i, j)),
        out_shape=jax.ShapeDtypeStruct((M, N), out_dtype),
        compiler_params=_params(("parallel", "arbitrary"), blocks, _nbytes((bm, bn), F32)),
        name="mm",
    )(x, w)


def _nt_dot(w, h):
    return lax.dot_general(w, h, (((1,), (1,)), ((), ())), preferred_element_type=F32)


def _proj_vt_kernel(w_ref, h_ref, o_ref):
    o_ref[...] = _nt_dot(w_ref[...], h_ref[...]).astype(o_ref.dtype)


def _proj_qt_kernel(w_ref, h_ref, cos_ref, sin_ref, o_ref, *, head_dim, rot_half, scale):
    acc = _nt_dot(w_ref[...], h_ref[...])
    cos = cos_ref[...]
    sin = sin_ref[...]
    for g in range(acc.shape[0] // head_dim):
        base = g * head_dim
        x1 = acc[base:base + rot_half]
        x2 = acc[base + rot_half:base + 2 * rot_half]
        rest = acc[base + 2 * rot_half:base + head_dim]
        rot = jnp.concatenate([x1 * cos - x2 * sin, x2 * cos + x1 * sin, rest], axis=0)
        o_ref[base:base + head_dim, :] = (rot * scale).astype(o_ref.dtype)


def _proj_t(h3, wt, rope=None):
    B, S, K = h3.shape
    N = wt.shape[0]
    bm = _pick(S, 1024, V7X_LANES)
    bn = _pick(N, 1024, V7X_LANES)
    blocks = _nbytes((bm, K), BF16) + _nbytes((bn, K), BF16) + _nbytes((bn, bm), BF16)
    in_specs = [pl.BlockSpec((bn, K), lambda b, i, j: (j, 0)),
                pl.BlockSpec((None, bm, K), lambda b, i, j: (b, i, 0))]
    args = [wt, h3]
    if rope is None:
        body = _proj_vt_kernel
    else:
        cos_t, sin_t, head_dim, scale = rope
        rot_half = cos_t.shape[0]
        body = functools.partial(_proj_qt_kernel, head_dim=head_dim, rot_half=rot_half, scale=scale)
        in_specs += [pl.BlockSpec((rot_half, bm), lambda b, i, j: (0, i)),
                     pl.BlockSpec((rot_half, bm), lambda b, i, j: (0, i))]
        args += [cos_t, sin_t]
    return pl.pallas_call(
        body,
        grid=(B, S // bm, N // bn),
        in_specs=in_specs,
        out_specs=pl.BlockSpec((None, bn, bm), lambda b, i, j: (b, j, i)),
        out_shape=jax.ShapeDtypeStruct((B, N, S), BF16),
        compiler_params=_params(("parallel", "parallel", "arbitrary"), blocks,
                                2 * _nbytes((bn, bm), F32)),
        name="proj_t",
    )(*args)


def _proj_k_kernel(h_ref, w_ref, c_ref, s1_ref, s2_ref, o_ref, *, rot_half):
    acc = jnp.dot(h_ref[...], w_ref[...], preferred_element_type=F32)
    c = c_ref[...]
    s1 = s1_ref[...]
    s2 = s2_ref[...]
    for t in range(acc.shape[1] // V7X_LANES):
        x = acc[:, t * V7X_LANES:(t + 1) * V7X_LANES]
        up = pltpu.roll(x, V7X_LANES - rot_half, 1)
        dn = pltpu.roll(x, rot_half, 1)
        o_ref[:, t * V7X_LANES:(t + 1) * V7X_LANES] = (x * c + up * s1 + dn * s2).astype(o_ref.dtype)


def _proj_k(h3, w, col0, N, c, s1, s2, rot_half):
    B, S, K = h3.shape
    bm = _pick(S, 1024, V7X_SUBLANES)
    bn = _pick(math.gcd(N, col0) if col0 else N, 1024, V7X_LANES)
    j0 = col0 // bn
    blocks = (_nbytes((bm, K), BF16) + _nbytes((K, bn), BF16) + _nbytes((bm, bn), BF16)
              + 3 * _nbytes((bm, V7X_LANES), F32))
    tab = pl.BlockSpec((bm, V7X_LANES), lambda b, i, j: (i, 0))
    return pl.pallas_call(
        functools.partial(_proj_k_kernel, rot_half=rot_half),
        grid=(B, S // bm, N // bn),
        in_specs=[pl.BlockSpec((None, bm, K), lambda b, i, j: (b, i, 0)),
                  pl.BlockSpec((K, bn), lambda b, i, j: (0, j0 + j)),
                  tab, tab, tab],
        out_specs=pl.BlockSpec((None, bm, bn), lambda b, i, j: (b, i, j)),
        out_shape=jax.ShapeDtypeStruct((B, S, N), BF16),
        compiler_params=_params(("parallel", "parallel", "arbitrary"), blocks,
                                2 * _nbytes((bm, bn), F32)),
        name="proj_k",
    )(h3, w, c, s1, s2)


def _rope_tables(S, head_dim):
    rot_dim = head_dim // 4
    rot_half = rot_dim // 2
    inv_freq = ROPE_THETA ** (-jnp.arange(0, rot_dim, 2, dtype=F32) / rot_dim)
    ang = jnp.arange(S).astype(F32)[:, None] * inv_freq[None, :]
    cos, sin = jnp.cos(ang), jnp.sin(ang)
    d = jnp.arange(V7X_LANES) % head_dim
    idx = d % rot_half
    lo = d < rot_half
    hi = (d >= rot_half) & (d < rot_dim)
    c = jnp.where((lo | hi)[None, :], cos[:, idx], 1.0)
    s1 = jnp.where(lo[None, :], -sin[:, idx], 0.0)
    s2 = jnp.where(hi[None, :], sin[:, idx], 0.0)
    return cos.T, sin.T, c, s1, s2


def _attn_kernel(lam_ref, qt_ref, k_ref, vt_ref, gain_ref, o_ref,
                 qbd_ref, vte_ref, sa_ref, sb_ref, acc1_ref, acc2_ref, *, tk, head_dim):
    tq = qt_ref.shape[1]
    S = k_ref.shape[0]
    hw = 2 * head_dim
    n_chunks = S // tk

    @pl.when(pl.program_id(2) == 0)
    def _():
        vte_ref[0:hw, :] = vt_ref[...]
        vte_ref[hw:, :] = jnp.ones((vte_ref.shape[0] - hw, S), vte_ref.dtype)

    qbd_ref[...] = jnp.zeros_like(qbd_ref)
    qbd_ref[0:head_dim, 0:tq] = qt_ref[0:head_dim, :]
    qbd_ref[head_dim:hw, tq:2 * tq] = qt_ref[head_dim:hw, :]
    acc1_ref[...] = jnp.zeros_like(acc1_ref)
    acc2_ref[...] = jnp.zeros_like(acc2_ref)

    def scores(c, s_ref):
        start = pl.multiple_of(c * tk, tk)
        s = jnp.dot(k_ref[pl.ds(start, tk), :], qbd_ref[...], preferred_element_type=F32)
        s_ref[...] = s
        return jnp.max(s, axis=0, keepdims=True)

    def accumulate(c, s_ref, cmax, m1, m2):
        start = pl.multiple_of(c * tk, tk)
        v = vte_ref[:, pl.ds(start, tk)]
        ms = []
        for m, acc_ref, cols in ((m1, acc1_ref, slice(0, tq)), (m2, acc2_ref, slice(tq, 2 * tq))):
            m_new = jnp.maximum(m, cmax[:, cols])
            alpha = jnp.exp2(m - m_new)
            p = jnp.exp2(s_ref[:, cols] - m_new).astype(BF16)
            acc_ref[...] = alpha * acc_ref[...] + jnp.dot(v, p, preferred_element_type=F32)
            ms.append(m_new)
        return ms

    per_trip = next(u for u in (8, 4, 2) if n_chunks % u == 0)
    bufs = (sa_ref, sb_ref)

    def body(j, carry):
        m1, m2, cmax = carry
        c = per_trip * j
        for i in range(per_trip):
            cmax_next = scores(jnp.minimum(c + i + 1, n_chunks - 1), bufs[(i + 1) % 2])
            m1, m2 = accumulate(c + i, bufs[i % 2], cmax, m1, m2)
            cmax = cmax_next
        return m1, m2, cmax

    neg = jnp.full((1, tq), -jnp.inf, F32)
    lax.fori_loop(0, n_chunks // per_trip, body, (neg, neg, scores(0, sa_ref)))

    lq1, lk1, lq2, lk2 = (lam_ref[i:i + 1, :] for i in range(4))
    lam = (jnp.exp(jnp.sum(lq1 * lk1, axis=-1, keepdims=True))
           - jnp.exp(jnp.sum(lq2 * lk2, axis=-1, keepdims=True)) + LAMBDA_INIT)
    o = (acc1_ref[0:hw, :] / acc1_ref[hw:hw + 1, :]
         - lam * (acc2_ref[0:hw, :] / acc2_ref[hw:hw + 1, :]))
    ms = jnp.mean(o * o, axis=0, keepdims=True)
    o = o * lax.rsqrt(ms + EPS) * gain_ref[...] * (1.0 - LAMBDA_INIT)
    o_ref[...] = o.T.astype(o_ref.dtype)


ONES_ROWS = 2 * V7X_SUBLANES


def _diff_attention(qt, k, vt, lam_params, gain, head_dim):
    B, W, S = qt.shape
    hw = 2 * head_dim
    H = W // hw
    tq = _pick(S, 256, V7X_LANES)
    tk = _pick(S // 2, 512, V7X_LANES)
    blocks = (_nbytes((hw, tq), BF16) + 2 * _nbytes((S, hw), BF16) + _nbytes((tq, hw), BF16))
    scratch = (_nbytes((hw, 2 * tq), BF16) + _nbytes((hw + ONES_ROWS, S), BF16)
               + 2 * _nbytes((tk, 2 * tq), F32) + 2 * _nbytes((hw + ONES_ROWS, tq), F32))
    temps = 2 * _nbytes((tk, 2 * tq), F32)
    return pl.pallas_call(
        functools.partial(_attn_kernel, tk=tk, head_dim=head_dim),
        grid=(B, H, S // tq),
        in_specs=[pl.BlockSpec((4, head_dim), lambda b, h, i: (0, 0)),
                  pl.BlockSpec((None, hw, tq), lambda b, h, i: (b, h, i)),
                  pl.BlockSpec((None, S, hw), lambda b, h, i: (b, 0, h)),
                  pl.BlockSpec((None, hw, S), lambda b, h, i: (b, h, 0)),
                  pl.BlockSpec((hw, 1), lambda b, h, i: (0, 0))],
        out_specs=pl.BlockSpec((None, tq, hw), lambda b, h, i: (b, i, h)),
        out_shape=jax.ShapeDtypeStruct((B, S, W), BF16),
        scratch_shapes=[pltpu.VMEM((hw, 2 * tq), BF16),
                        pltpu.VMEM((hw + ONES_ROWS, S), BF16),
                        pltpu.VMEM((tk, 2 * tq), F32),
                        pltpu.VMEM((tk, 2 * tq), F32),
                        pltpu.VMEM((hw + ONES_ROWS, tq), F32),
                        pltpu.VMEM((hw + ONES_ROWS, tq), F32)],
        compiler_params=_params(("arbitrary", "arbitrary", "arbitrary"), blocks, scratch + temps),
        name="diff_attention",
    )(lam_params, qt, k, vt, gain.reshape(hw, 1).astype(F32))


def _softplus(x):
    return jnp.maximum(x, 0.0) + jnp.log1p(jnp.exp(-jnp.abs(x)))


def _rglru_kernel(*refs, reverse, n_chunks, conv_width):
    if reverse:
        (x_ref, xp_ref, xn_ref, cw_ref, cb_ref, w_ref, b_ref, lam_ref, hf_ref, y_ref,
         o_ref, xe_ref, a_ref, u_ref, h_ref) = refs
    else:
        (x_ref, xp_ref, xn_ref, cw_ref, cb_ref, w_ref, b_ref, lam_ref,
         o_ref, xe_ref, a_ref, u_ref, h_ref) = refs
    T = x_ref.shape[0]
    halo = xp_ref.shape[0]
    nblk, bw = w_ref.shape[0], w_ref.shape[1]
    step = pl.program_id(1)
    chunk = (n_chunks - 1 - step) if reverse else step

    xe_ref[0:halo, :] = jnp.where(chunk == 0, 0.0, xp_ref[...])
    xe_ref[halo:halo + T, :] = x_ref[...]
    xe_ref[halo + T:2 * halo + T, :] = jnp.where(chunk == n_chunks - 1, 0.0, xn_ref[...])
    left = conv_width // 2
    xc = cb_ref[...] + cw_ref[0:1, :] * xe_ref[halo - left:halo - left + T, :]
    for j in range(1, conv_width):
        xc = xc + cw_ref[j:j + 1, :] * xe_ref[halo - left + j:halo - left + j + T, :]

    sp = _softplus(-lam_ref[...])
    for n in range(nblk):
        cols = slice(n * bw, (n + 1) * bw)
        xb = xc[:, cols]
        z = jnp.dot(xb.astype(BF16), w_ref[n], preferred_element_type=F32) + b_ref[n]
        r = jax.nn.sigmoid(z[:, :bw])
        i = jax.nn.sigmoid(z[:, bw:])
        log_a = -RG_C * r * sp[:, cols]
        a_ref[:, cols] = jnp.exp(log_a)
        g = jnp.maximum(1.0 - jnp.exp(2.0 * log_a), 0.0)
        gain = g * lax.rsqrt(jnp.maximum(g, jnp.finfo(F32).tiny))
        u_ref[:, cols] = gain * (i * xb)

    @pl.when(step == 0)
    def _():
        h_ref[...] = jnp.zeros_like(h_ref)

    def group(g, h):
        gi = (T // V7X_SUBLANES - 1 - g) if reverse else g
        base = pl.multiple_of(gi * V7X_SUBLANES, V7X_SUBLANES)
        order = range(V7X_SUBLANES - 1, -1, -1) if reverse else range(V7X_SUBLANES)
        for r_ in order:
            row = pl.ds(base + r_, 1)
            h = a_ref[row, :] * h + u_ref[row, :]
            u_ref[row, :] = h
        return h

    h_ref[...] = lax.fori_loop(0, T // V7X_SUBLANES, group, h_ref[...])

    if reverse:
        o_ref[...] = ((hf_ref[...] + u_ref[...]) * jax.nn.gelu(y_ref[...])).astype(o_ref.dtype)
    else:
        o_ref[...] = u_ref[...]


def _rglru_pass(rest, x_col, y_col, conv_w, conv_b, wcat, bcat, lam, hf, reverse):
    B, S, _ = rest.shape
    nblk, bw, _ = wcat.shape
    C = nblk * bw
    conv_width = conv_w.shape[0]
    T = _pick(S, 256, V7X_SUBLANES)
    halo = V7X_SUBLANES
    n_chunks = S // T
    hpc = T // halo

    def cidx(c):
        return (n_chunks - 1 - c) if reverse else c

    in_specs = [
        pl.BlockSpec((None, T, C), lambda b, c: (b, cidx(c), x_col)),
        pl.BlockSpec((None, halo, C),
                     lambda b, c: (b, jnp.maximum(cidx(c) * hpc - 1, 0), x_col)),
        pl.BlockSpec((None, halo, C),
                     lambda b, c: (b, jnp.minimum((cidx(c) + 1) * hpc, S // halo - 1), x_col)),
        pl.BlockSpec((conv_width, C), lambda b, c: (0, 0)),
        pl.BlockSpec((1, C), lambda b, c: (0, 0)),
        pl.BlockSpec((nblk, bw, 2 * bw), lambda b, c: (0, 0, 0)),
        pl.BlockSpec((nblk, 1, 2 * bw), lambda b, c: (0, 0, 0)),
        pl.BlockSpec((1, C), lambda b, c: (0, 0)),
    ]
    args = [rest, rest, rest, conv_w, conv_b, wcat, bcat, lam]
    blocks = 2 * _nbytes((T, C), F32) + 2 * _nbytes((halo, C), F32) + 2 * _nbytes(wcat.shape, BF16)
    if reverse:
        in_specs += [pl.BlockSpec((None, T, C), lambda b, c: (b, cidx(c), 0)),
                     pl.BlockSpec((None, T, C), lambda b, c: (b, cidx(c), y_col))]
        args += [hf, rest]
        blocks += 2 * _nbytes((T, C), F32)
        out_dtype = BF16
    else:
        out_dtype = F32
    scratch = [pltpu.VMEM((T + 2 * halo, C), F32), pltpu.VMEM((T, C), F32),
               pltpu.VMEM((T, C), F32), pltpu.VMEM((1, C), F32)]
    return pl.pallas_call(
        functools.partial(_rglru_kernel, reverse=reverse, n_chunks=n_chunks, conv_width=conv_width),
        grid=(B, n_chunks),
        in_specs=in_specs,
        out_specs=pl.BlockSpec((None, T, C), lambda b, c: (b, cidx(c), 0)),
        out_shape=jax.ShapeDtypeStruct((B, S, C), out_dtype),
        scratch_shapes=scratch,
        compiler_params=_params(("parallel", "arbitrary"), blocks, 8 * _nbytes((T, C), F32)),
        name="rglru_bwd" if reverse else "rglru_fwd",
    )(*args)


def _merge_kernel(o_ref, r_ref, wa_ref, wr_ref, ga_ref, gr_ref, out_ref):
    a = jnp.dot(o_ref[...], wa_ref[...], preferred_element_type=F32)
    r = jnp.dot(r_ref[...], wr_ref[...], preferred_element_type=F32)
    out_ref[...] = (jax.nn.sigmoid(ga_ref[...]) * a + jax.nn.sigmoid(gr_ref[...]) * r).astype(out_ref.dtype)


def _merge(o, rr, wa, wr, rest, gate_col0):
    M, K = o.shape
    N = wa.shape[1]
    bm = _pick(M, 1024, V7X_SUBLANES)
    bn = _pick(N, 512, V7X_LANES)
    nb = N // bn
    ga0 = gate_col0 // bn
    blocks = (2 * _nbytes((bm, K), BF16) + 2 * _nbytes((K, bn), BF16) + 2 * _nbytes((bm, bn), F32)
              + _nbytes((bm, bn), BF16))
    return pl.pallas_call(
        _merge_kernel,
        grid=(M // bm, nb),
        in_specs=[pl.BlockSpec((bm, K), lambda i, j: (i, 0)),
                  pl.BlockSpec((bm, K), lambda i, j: (i, 0)),
                  pl.BlockSpec((K, bn), lambda i, j: (0, j)),
                  pl.BlockSpec((K, bn), lambda i, j: (0, j)),
                  pl.BlockSpec((bm, bn), lambda i, j: (i, ga0 + j)),
                  pl.BlockSpec((bm, bn), lambda i, j: (i, ga0 + nb + j))],
        out_specs=pl.BlockSpec((bm, bn), lambda i, j: (i, j)),
        out_shape=jax.ShapeDtypeStruct((M, N), BF16),
        compiler_params=_params(("parallel", "arbitrary"), blocks, 3 * _nbytes((bm, bn), F32)),
        name="merge",
    )(o, rr, wa, wr, rest, rest)


def _prepare_weights(ffn1_w_in, ffn1_w_out, w_in, conv_w, conv_b, rg_wa, rg_ba, rg_wx, rg_bx, rg_lambda,
                     w_attn_proj, w_rnn_proj, w_out, ffn2_w_in, ffn2_w_out, qk_width, rnn_width):
    def ffn_w(w_i, w_o):
        D, F = w_i.shape[0], w_o.shape[0]
        Fp = -(-F // 1024) * 1024
        pad = Fp - F
        w_gu = jnp.zeros((D, 2 * Fp), BF16)
        w_gu = lax.dynamic_update_slice(w_gu, w_i[:, :F].astype(BF16), (0, 0))
        w_gu = lax.dynamic_update_slice(w_gu, w_i[:, F:].astype(BF16), (0, Fp))
        wo = jnp.pad(w_o.astype(BF16), ((0, pad), (0, 0)))
        return w_gu, wo

    c1, c2, c3 = qk_width, 2 * qk_width, 3 * qk_width
    wi = w_in.astype(BF16)
    nblk, bw = rg_wa.shape[1], rg_wa.shape[2]
    return dict(
        ffn1=ffn_w(ffn1_w_in[0], ffn1_w_out[0]),
        ffn2=ffn_w(ffn2_w_in[0], ffn2_w_out[0]),
        w_in=wi, wq_t=wi[:, :c1].T, wv_t=wi[:, c2:c3].T,
        conv_w=conv_w.astype(F32), conv_b=conv_b.reshape(1, rnn_width).astype(F32),
        wcat=jnp.concatenate([rg_wa, rg_wx], axis=-1).astype(BF16),
        bcat=jnp.concatenate([rg_ba, rg_bx], axis=-1).reshape(2, nblk, 1, 2 * bw).astype(F32),
        lam=rg_lambda.reshape(2, 1, rnn_width).astype(F32),
        w_attn=w_attn_proj.astype(BF16), w_rnn=w_rnn_proj.astype(BF16), w_out=w_out.astype(BF16),
    )


def _trunk(x, p, ffn1_norm, mix_norm, lam_params, subln_gain, ffn2_norm, final_norm, head_dim, rnn_width):
    B, S, D = x.shape
    M = B * S
    x = x.reshape(M, D)
    x = _ffn(x, ffn1_norm, *p["ffn1"])

    h = _rmsnorm(x, mix_norm, BF16)
    h3 = h.reshape(B, S, D)
    cos_t, sin_t, c, s1, s2 = _rope_tables(S, head_dim)
    qt = _proj_t(h3, p["wq_t"], rope=(cos_t, sin_t, head_dim, head_dim ** -0.5 * math.log2(math.e)))
    qk_width = p["wq_t"].shape[0]
    k = _proj_k(h3, p["w_in"], qk_width, qk_width, c, s1, s2, cos_t.shape[0])
    vt = _proj_t(h3, p["wv_t"])
    c3 = 3 * qk_width
    rest = _mm(h, p["w_in"], c3, p["w_in"].shape[1] - c3, F32)

    o = _diff_attention(qt, k, vt, lam_params, subln_gain, head_dim)

    rest3 = rest.reshape(B, S, rest.shape[1])
    rnn = (rest3, 0, 1, p["conv_w"], p["conv_b"])
    hf = _rglru_pass(*rnn, p["wcat"][0], p["bcat"][0], p["lam"][0], None, reverse=False)
    rr = _rglru_pass(*rnn, p["wcat"][1], p["bcat"][1], p["lam"][1], hf, reverse=True)

    merged = _merge(o.reshape(M, -1), rr.reshape(M, rnn_width), p["w_attn"], p["w_rnn"], rest, 2 * rnn_width)
    x = _mm_residual(merged, p["w_out"], x, 1.0)

    x = _ffn(x, ffn2_norm, *p["ffn2"])
    return _rmsnorm(x, final_norm, F32).reshape(B, S, D)


def kernel(x_prompt, x_sample, ffn1_norm, ffn1_w_in, ffn1_w_out, mix_norm, w_in, lambda_q1, lambda_k1, lambda_q2, lambda_k2, subln_gain, conv_w, conv_b, rg_wa, rg_ba, rg_wx, rg_bx, rg_lambda, w_attn_proj, w_rnn_proj, w_out, ffn2_norm, ffn2_w_in, ffn2_w_out, final_norm):
    head_dim = lambda_q1.shape[-1]
    rnn_width = conv_w.shape[-1]
    D = x_prompt.shape[-1]
    qk_width = (w_in.shape[-1] - 2 * rnn_width - 2 * D) // 3
    p = _prepare_weights(ffn1_w_in, ffn1_w_out, w_in[0], conv_w[0], conv_b[0], rg_wa[0], rg_ba[0],
                         rg_wx[0], rg_bx[0], rg_lambda[0], w_attn_proj[0], w_rnn_proj[0], w_out[0],
                         ffn2_w_in, ffn2_w_out, qk_width, rnn_width)
    lam_params = jnp.concatenate([lambda_q1, lambda_k1, lambda_q2, lambda_k2], axis=0).astype(F32)
    args = (p, ffn1_norm[0], mix_norm[0], lam_params, subln_gain[0], ffn2_norm[0], final_norm,
            head_dim, rnn_width)
    return _trunk(x_prompt, *args), _trunk(x_sample, *args)
```

```python
import functools
import math

import jax
import jax.numpy as jnp
from jax import lax
from jax.experimental import pallas as pl
from jax.experimental.pallas import tpu as pltpu

EPS = 1e-6
ROPE_THETA = 500000.0
RG_C = 8.0
LAMBDA_INIT = 0.8 - 0.6 * math.exp(-0.3 * 0)

V7X_LANES = 128
V7X_SUBLANES = 8
V7X_VMEM_BYTES = 64 * 1024 * 1024
VMEM_CAP = V7X_VMEM_BYTES - 6 * 1024 * 1024

F32 = jnp.float32
BF16 = jnp.bfloat16


def _pick(n, target, align):
    best = None
    for d in range(align, min(n, target) + 1, align):
        if n % d == 0:
            best = d
    return best if best is not None else n


def _nbytes(shape, dtype):
    return math.prod(shape) * jnp.dtype(dtype).itemsize


def _params(semantics, pipelined_bytes, resident_bytes=0):
    need = 2 * pipelined_bytes + resident_bytes + 4 * 1024 * 1024
    return pltpu.CompilerParams(dimension_semantics=semantics,
                                vmem_limit_bytes=int(min(max(need, 16 * 1024 * 1024), VMEM_CAP)))


def _rmsnorm_kernel(x_ref, g_ref, o_ref):
    x = x_ref[...]
    ms = jnp.mean(x * x, axis=-1, keepdims=True)
    o_ref[...] = (x * lax.rsqrt(ms + EPS) * g_ref[...]).astype(o_ref.dtype)


def _rmsnorm(x, g, out_dtype):
    M, D = x.shape
    bm = _pick(M, 256, V7X_SUBLANES)
    blocks = _nbytes((bm, D), F32) + _nbytes((bm, D), out_dtype)
    return pl.pallas_call(
        _rmsnorm_kernel,
        grid=(M // bm,),
        in_specs=[pl.BlockSpec((bm, D), lambda i: (i, 0)),
                  pl.BlockSpec((1, D), lambda i: (0, 0))],
        out_specs=pl.BlockSpec((bm, D), lambda i: (i, 0)),
        out_shape=jax.ShapeDtypeStruct((M, D), out_dtype),
        compiler_params=_params(("parallel",), blocks, _nbytes((bm, D), F32)),
        name="rmsnorm",
    )(x, g.reshape(1, D).astype(F32))


def _ffn_up_kernel(h_ref, wg_ref, wu_ref, o_ref):
    h = h_ref[...]
    g = jnp.dot(h, wg_ref[...], preferred_element_type=F32)
    u = jnp.dot(h, wu_ref[...], preferred_element_type=F32)
    o_ref[...] = (g * jax.nn.sigmoid(g) * u).astype(o_ref.dtype)


def _ffn_up(h, w_gu):
    M, D = h.shape
    Fp = w_gu.shape[1] // 2
    bm = _pick(M, 1024, V7X_SUBLANES)
    bf = _pick(Fp, 512, V7X_LANES)
    nf = Fp // bf
    blocks = _nbytes((bm, D), BF16) + 2 * _nbytes((D, bf), BF16) + _nbytes((bm, bf), BF16)
    return pl.pallas_call(
        _ffn_up_kernel,
        grid=(M // bm, nf),
        in_specs=[pl.BlockSpec((bm, D), lambda i, j: (i, 0)),
                  pl.BlockSpec((D, bf), lambda i, j: (0, j)),
                  pl.BlockSpec((D, bf), lambda i, j: (0, nf + j))],
        out_specs=pl.BlockSpec((bm, bf), lambda i, j: (i, j)),
        out_shape=jax.ShapeDtypeStruct((M, Fp), BF16),
        compiler_params=_params(("parallel", "arbitrary"), blocks, 3 * _nbytes((bm, bf), F32)),
        name="ffn_up",
    )(h, w_gu, w_gu)


def _mm_res_kernel(a_ref, w_ref, r_ref, o_ref, *, scale):
    part = scale * jnp.dot(a_ref[...], w_ref[...], preferred_element_type=F32)

    @pl.when(pl.program_id(2) == 0)
    def _():
        o_ref[...] = r_ref[...] + part

    @pl.when(pl.program_id(2) != 0)
    def _():
        o_ref[...] += part


def _mm_residual(a, w, res, scale):
    M, K = a.shape
    N = w.shape[1]
    def footprint(bm, bn, bk):
        return _nbytes((bm, bk), BF16) + _nbytes((bk, bn), BF16) + 2 * _nbytes((bm, bn), F32)

    for t in (1024, 512):
        bm, bn, bk = _pick(M, t, V7X_SUBLANES), _pick(N, t, V7X_LANES), K
        if 2 * footprint(bm, bn, bk) <= VMEM_CAP - 6 * 1024 * 1024:
            break
    else:
        bm, bn = _pick(M, 1024, V7X_SUBLANES), _pick(N, 1024, V7X_LANES)
        bk = _pick(K, 3072, 2 * V7X_LANES)
    blocks = footprint(bm, bn, bk)
    return pl.pallas_call(
        functools.partial(_mm_res_kernel, scale=scale),
        grid=(M // bm, N // bn, K // bk),
        in_specs=[pl.BlockSpec((bm, bk), lambda i, j, k: (i, k)),
                  pl.BlockSpec((bk, bn), lambda i, j, k: (k, j)),
                  pl.BlockSpec((bm, bn), lambda i, j, k: (i, j))],
        out_specs=pl.BlockSpec((bm, bn), lambda i, j, k: (i, j)),
        out_shape=jax.ShapeDtypeStruct((M, N), F32),
        compiler_params=_params(("parallel", "parallel", "arbitrary"), blocks,
                                2 * _nbytes((bm, bn), F32)),
        name="mm_residual",
    )(a, w, res)


def _ffn(x, norm_g, w_gu, wo):
    h = _rmsnorm(x, norm_g, BF16)
    a = _ffn_up(h, w_gu)
    return _mm_residual(a, wo, x, 0.5)


def _mm_kernel(x_ref, w_ref, o_ref):
    o_ref[...] = jnp.dot(x_ref[...], w_ref[...], preferred_element_type=F32).astype(o_ref.dtype)


def _mm(x, w, col0, N, out_dtype):
    M, K = x.shape
    bm = _pick(M, 1024, V7X_SUBLANES)
    bn = _pick(math.gcd(N, col0) if col0 else N, 1024, V7X_LANES)
    j0 = col0 // bn
    blocks = _nbytes((bm, K), BF16) + _nbytes((K, bn), BF16) + _nbytes((bm, bn), out_dtype)
    return pl.pallas_call(
        _mm_kernel,
        grid=(M // bm, N // bn),
        in_specs=[pl.BlockSpec((bm, K), lambda i, j: (i, 0)),
                  pl.BlockSpec((K, bn), lambda i, j: (0, j0 + j))],
        out_specs=pl.BlockSpec((bm, bn), lambda i, j: (i, j)),
        out_shape=jax.ShapeDtypeStruct((M, N), out_dtype),
        compiler_params=_params(("parallel", "arbitrary"), blocks, _nbytes((bm, bn), F32)),
        name="mm",
    )(x, w)


def _nt_dot(w, h):
    return lax.dot_general(w, h, (((1,), (1,)), ((), ())), preferred_element_type=F32)


def _proj_vt_kernel(w_ref, h_ref, o_ref):
    o_ref[...] = _nt_dot(w_ref[...], h_ref[...]).astype(o_ref.dtype)


def _proj_qt_kernel(w_ref, h_ref, cos_ref, sin_ref, o_ref, *, head_dim, rot_half, scale):
    acc = _nt_dot(w_ref[...], h_ref[...])
    cos = cos_ref[...]
    sin = sin_ref[...]
    for g in range(acc.shape[0] // head_dim):
        base = g * head_dim
        x1 = acc[base:base + rot_half]
        x2 = acc[base + rot_half:base + 2 * rot_half]
        rest = acc[base + 2 * rot_half:base + head_dim]
        rot = jnp.concatenate([x1 * cos - x2 * sin, x2 * cos + x1 * sin, rest], axis=0)
        o_ref[base:base + head_dim, :] = (rot * scale).astype(o_ref.dtype)


def _proj_t(h3, wt, rope=None):
    B, S, K = h3.shape
    N = wt.shape[0]
    bm = _pick(S, 1024, V7X_LANES)
    bn = _pick(N, 1024, V7X_LANES)
    blocks = _nbytes((bm, K), BF16) + _nbytes((bn, K), BF16) + _nbytes((bn, bm), BF16)
    in_specs = [pl.BlockSpec((bn, K), lambda b, i, j: (j, 0)),
                pl.BlockSpec((None, bm, K), lambda b, i, j: (b, i, 0))]
    args = [wt, h3]
    if rope is None:
        body = _proj_vt_kernel
    else:
        cos_t, sin_t, head_dim, scale = rope
        rot_half = cos_t.shape[0]
        body = functools.partial(_proj_qt_kernel, head_dim=head_dim, rot_half=rot_half, scale=scale)
        in_specs += [pl.BlockSpec((rot_half, bm), lambda b, i, j: (0, i)),
                     pl.BlockSpec((rot_half, bm), lambda b, i, j: (0, i))]
        args += [cos_t, sin_t]
    return pl.pallas_call(
        body,
        grid=(B, S // bm, N // bn),
        in_specs=in_specs,
        out_specs=pl.BlockSpec((None, bn, bm), lambda b, i, j: (b, j, i)),
        out_shape=jax.ShapeDtypeStruct((B, N, S), BF16),
        compiler_params=_params(("parallel", "parallel", "arbitrary"), blocks,
                                2 * _nbytes((bn, bm), F32)),
        name="proj_t",
    )(*args)


def _proj_k_kernel(h_ref, w_ref, c_ref, s1_ref, s2_ref, o_ref, *, rot_half):
    acc = jnp.dot(h_ref[...], w_ref[...], preferred_element_type=F32)
    c = c_ref[...]
    s1 = s1_ref[...]
    s2 = s2_ref[...]
    for t in range(acc.shape[1] // V7X_LANES):
        x = acc[:, t * V7X_LANES:(t + 1) * V7X_LANES]
        up = pltpu.roll(x, V7X_LANES - rot_half, 1)
        dn = pltpu.roll(x, rot_half, 1)
        o_ref[:, t * V7X_LANES:(t + 1) * V7X_LANES] = (x * c + up * s1 + dn * s2).astype(o_ref.dtype)


def _proj_k(h3, w, col0, N, c, s1, s2, rot_half):
    B, S, K = h3.shape
    bm = _pick(S, 1024, V7X_SUBLANES)
    bn = _pick(math.gcd(N, col0) if col0 else N, 1024, V7X_LANES)
    j0 = col0 // bn
    blocks = (_nbytes((bm, K), BF16) + _nbytes((K, bn), BF16) + _nbytes((bm, bn), BF16)
              + 3 * _nbytes((bm, V7X_LANES), F32))
    tab = pl.BlockSpec((bm, V7X_LANES), lambda b, i, j: (i, 0))
    return pl.pallas_call(
        functools.partial(_proj_k_kernel, rot_half=rot_half),
        grid=(B, S // bm, N // bn),
        in_specs=[pl.BlockSpec((None, bm, K), lambda b, i, j: (b, i, 0)),
                  pl.BlockSpec((K, bn), lambda b, i, j: (0, j0 + j)),
                  tab, tab, tab],
        out_specs=pl.BlockSpec((None, bm, bn), lambda b, i, j: (b, i, j)),
        out_shape=jax.ShapeDtypeStruct((B, S, N), BF16),
        compiler_params=_params(("parallel", "parallel", "arbitrary"), blocks,
                                2 * _nbytes((bm, bn), F32)),
        name="proj_k",
    )(h3, w, c, s1, s2)


def _rope_tables(S, head_dim):
    rot_dim = head_dim // 4
    rot_half = rot_dim // 2
    inv_freq = ROPE_THETA ** (-jnp.arange(0, rot_dim, 2, dtype=F32) / rot_dim)
    ang = jnp.arange(S).astype(F32)[:, None] * inv_freq[None, :]
    cos, sin = jnp.cos(ang), jnp.sin(ang)
    d = jnp.arange(V7X_LANES) % head_dim
    idx = d % rot_half
    lo = d < rot_half
    hi = (d >= rot_half) & (d < rot_dim)
    c = jnp.where((lo | hi)[None, :], cos[:, idx], 1.0)
    s1 = jnp.where(lo[None, :], -sin[:, idx], 0.0)
    s2 = jnp.where(hi[None, :], sin[:, idx], 0.0)
    return cos.T, sin.T, c, s1, s2


def _attn_kernel(lam_ref, qt_ref, k_ref, vt_ref, gain_ref, o_ref,
                 qbd_ref, vte_ref, sa_ref, sb_ref, acc1_ref, acc2_ref, *, tk, head_dim):
    tq = qt_ref.shape[1]
    S = k_ref.shape[0]
    hw = 2 * head_dim
    n_chunks = S // tk

    @pl.when(pl.program_id(2) == 0)
    def _():
        vte_ref[0:hw, :] = vt_ref[...]
        vte_ref[hw:, :] = jnp.ones((vte_ref.shape[0] - hw, S), vte_ref.dtype)

    qbd_ref[...] = jnp.zeros_like(qbd_ref)
    qbd_ref[0:head_dim, 0:tq] = qt_ref[0:head_dim, :]
    qbd_ref[head_dim:hw, tq:2 * tq] = qt_ref[head_dim:hw, :]
    acc1_ref[...] = jnp.zeros_like(acc1_ref)
    acc2_ref[...] = jnp.zeros_like(acc2_ref)

    def scores(c, s_ref):
        start = pl.multiple_of(c * tk, tk)
        s = jnp.dot(k_ref[pl.ds(start, tk), :], qbd_ref[...], preferred_element_type=F32)
        s_ref[...] = s
        return jnp.max(s, axis=0, keepdims=True)

    def accumulate(c, s_ref, cmax, m1, m2):
        start = pl.multiple_of(c * tk, tk)
        v = vte_ref[:, pl.ds(start, tk)]
        ms = []
        for m, acc_ref, cols in ((m1, acc1_ref, slice(0, tq)), (m2, acc2_ref, slice(tq, 2 * tq))):
            m_new = jnp.maximum(m, cmax[:, cols])
            alpha = jnp.exp2(m - m_new)
            p = jnp.exp2(s_ref[:, cols] - m_new).astype(BF16)
            acc_ref[...] = alpha * acc_ref[...] + jnp.dot(v, p, preferred_element_type=F32)
            ms.append(m_new)
        return ms

    per_trip = next(u for u in (8, 4, 2) if n_chunks % u == 0)
    bufs = (sa_ref, sb_ref)

    def body(j, carry):
        m1, m2, cmax = carry
        c = per_trip * j
        for i in range(per_trip):
            cmax_next = scores(jnp.minimum(c + i + 1, n_chunks - 1), bufs[(i + 1) % 2])
            m1, m2 = accumulate(c + i, bufs[i % 2], cmax, m1, m2)
            cmax = cmax_next
        return m1, m2, cmax

    neg = jnp.full((1, tq), -jnp.inf, F32)
    lax.fori_loop(0, n_chunks // per_trip, body, (neg, neg, scores(0, sa_ref)))

    lq1, lk1, lq2, lk2 = (lam_ref[i:i + 1, :] for i in range(4))
    lam = (jnp.exp(jnp.sum(lq1 * lk1, axis=-1, keepdims=True))
           - jnp.exp(jnp.sum(lq2 * lk2, axis=-1, keepdims=True)) + LAMBDA_INIT)
    o = (acc1_ref[0:hw, :] / acc1_ref[hw:hw + 1, :]
         - lam * (acc2_ref[0:hw, :] / acc2_ref[hw:hw + 1, :]))
    ms = jnp.mean(o * o, axis=0, keepdims=True)
    o = o * lax.rsqrt(ms + EPS) * gain_ref[...] * (1.0 - LAMBDA_INIT)
    o_ref[...] = o.T.astype(o_ref.dtype)


ONES_ROWS = 2 * V7X_SUBLANES


def _diff_attention(qt, k, vt, lam_params, gain, head_dim):
    B, W, S = qt.shape
    hw = 2 * head_dim
    H = W // hw
    tq = _pick(S, 256, V7X_LANES)
    tk = _pick(S // 2, 512, V7X_LANES)
    blocks = (_nbytes((hw, tq), BF16) + 2 * _nbytes((S, hw), BF16) + _nbytes((tq, hw), BF16))
    scratch = (_nbytes((hw, 2 * tq), BF16) + _nbytes((hw + ONES_ROWS, S), BF16)
               + 2 * _nbytes((tk, 2 * tq), F32) + 2 * _nbytes((hw + ONES_ROWS, tq), F32))
    temps = 2 * _nbytes((tk, 2 * tq), F32)
    return pl.pallas_call(
        functools.partial(_attn_kernel, tk=tk, head_dim=head_dim),
        grid=(B, H, S // tq),
        in_specs=[pl.BlockSpec((4, head_dim), lambda b, h, i: (0, 0)),
                  pl.BlockSpec((None, hw, tq), lambda b, h, i: (b, h, i)),
                  pl.BlockSpec((None, S, hw), lambda b, h, i: (b, 0, h)),
                  pl.BlockSpec((None, hw, S), lambda b, h, i: (b, h, 0)),
                  pl.BlockSpec((hw, 1), lambda b, h, i: (0, 0))],
        out_specs=pl.BlockSpec((None, tq, hw), lambda b, h, i: (b, i, h)),
        out_shape=jax.ShapeDtypeStruct((B, S, W), BF16),
        scratch_shapes=[pltpu.VMEM((hw, 2 * tq), BF16),
                        pltpu.VMEM((hw + ONES_ROWS, S), BF16),
                        pltpu.VMEM((tk, 2 * tq), F32),
                        pltpu.VMEM((tk, 2 * tq), F32),
                        pltpu.VMEM((hw + ONES_ROWS, tq), F32),
                        pltpu.VMEM((hw + ONES_ROWS, tq), F32)],
        compiler_params=_params(("arbitrary", "arbitrary", "arbitrary"), blocks, scratch + temps),
        name="diff_attention",
    )(lam_params, qt, k, vt, gain.reshape(hw, 1).astype(F32))


def _softplus(x):
    return jnp.maximum(x, 0.0) + jnp.log1p(jnp.exp(-jnp.abs(x)))


def _rglru_kernel(*refs, reverse, n_chunks, conv_width):
    if reverse:
        (x_ref, xp_ref, xn_ref, cw_ref, cb_ref, w_ref, b_ref, lam_ref, hf_ref, y_ref,
         o_ref, xe_ref, a_ref, u_ref, h_ref) = refs
    else:
        (x_ref, xp_ref, xn_ref, cw_ref, cb_ref, w_ref, b_ref, lam_ref,
         o_ref, xe_ref, a_ref, u_ref, h_ref) = refs
    T = x_ref.shape[0]
    halo = xp_ref.shape[0]
    nblk, bw = w_ref.shape[0], w_ref.shape[1]
    step = pl.program_id(1)
    chunk = (n_chunks - 1 - step) if reverse else step

    xe_ref[0:halo, :] = jnp.where(chunk == 0, 0.0, xp_ref[...])
    xe_ref[halo:halo + T, :] = x_ref[...]
    xe_ref[halo + T:2 * halo + T, :] = jnp.where(chunk == n_chunks - 1, 0.0, xn_ref[...])
    left = conv_width // 2
    xc = cb_ref[...] + cw_ref[0:1, :] * xe_ref[halo - left:halo - left + T, :]
    for j in range(1, conv_width):
        xc = xc + cw_ref[j:j + 1, :] * xe_ref[halo - left + j:halo - left + j + T, :]

    sp = _softplus(-lam_ref[...])
    for n in range(nblk):
        cols = slice(n * bw, (n + 1) * bw)
        xb = xc[:, cols]
        z = jnp.dot(xb.astype(BF16), w_ref[n], preferred_element_type=F32) + b_ref[n]
        r = jax.nn.sigmoid(z[:, :bw])
        i = jax.nn.sigmoid(z[:, bw:])
        log_a = -RG_C * r * sp[:, cols]
        a_ref[:, cols] = jnp.exp(log_a)
        g = jnp.maximum(1.0 - jnp.exp(2.0 * log_a), 0.0)
        gain = g * lax.rsqrt(jnp.maximum(g, jnp.finfo(F32).tiny))
        u_ref[:, cols] = gain * (i * xb)

    @pl.when(step == 0)
    def _():
        h_ref[...] = jnp.zeros_like(h_ref)

    def group(g, h):
        gi = (T // V7X_SUBLANES - 1 - g) if reverse else g
        base = pl.multiple_of(gi * V7X_SUBLANES, V7X_SUBLANES)
        order = range(V7X_SUBLANES - 1, -1, -1) if reverse else range(V7X_SUBLANES)
        for r_ in order:
            row = pl.ds(base + r_, 1)
            h = a_ref[row, :] * h + u_ref[row, :]
            u_ref[row, :] = h
        return h

    h_ref[...] = lax.fori_loop(0, T // V7X_SUBLANES, group, h_ref[...])

    if reverse:
        o_ref[...] = ((hf_ref[...] + u_ref[...]) * jax.nn.gelu(y_ref[...])).astype(o_ref.dtype)
    else:
        o_ref[...] = u_ref[...]


def _rglru_pass(rest, x_col, y_col, conv_w, conv_b, wcat, bcat, lam, hf, reverse):
    B, S, _ = rest.shape
    nblk, bw, _ = wcat.shape
    C = nblk * bw
    conv_width = conv_w.shape[0]
    T = _pick(S, 256, V7X_SUBLANES)
    halo = V7X_SUBLANES
    n_chunks = S // T
    hpc = T // halo

    def cidx(c):
        return (n_chunks - 1 - c) if reverse else c

    in_specs = [
        pl.BlockSpec((None, T, C), lambda b, c: (b, cidx(c), x_col)),
        pl.BlockSpec((None, halo, C),
                     lambda b, c: (b, jnp.maximum(cidx(c) * hpc - 1, 0), x_col)),
        pl.BlockSpec((None, halo, C),
                     lambda b, c: (b, jnp.minimum((cidx(c) + 1) * hpc, S // halo - 1), x_col)),
        pl.BlockSpec((conv_width, C), lambda b, c: (0, 0)),
        pl.BlockSpec((1, C), lambda b, c: (0, 0)),
        pl.BlockSpec((nblk, bw, 2 * bw), lambda b, c: (0, 0, 0)),
        pl.BlockSpec((nblk, 1, 2 * bw), lambda b, c: (0, 0, 0)),
        pl.BlockSpec((1, C), lambda b, c: (0, 0)),
    ]
    args = [rest, rest, rest, conv_w, conv_b, wcat, bcat, lam]
    blocks = 2 * _nbytes((T, C), F32) + 2 * _nbytes((halo, C), F32) + 2 * _nbytes(wcat.shape, BF16)
    if reverse:
        in_specs += [pl.BlockSpec((None, T, C), lambda b, c: (b, cidx(c), 0)),
                     pl.BlockSpec((None, T, C), lambda b, c: (b, cidx(c), y_col))]
        args += [hf, rest]
        blocks += 2 * _nbytes((T, C), F32)
        out_dtype = BF16
    else:
        out_dtype = F32
    scratch = [pltpu.VMEM((T + 2 * halo, C), F32), pltpu.VMEM((T, C), F32),
               pltpu.VMEM((T, C), F32), pltpu.VMEM((1, C), F32)]
    return pl.pallas_call(
        functools.partial(_rglru_kernel, reverse=reverse, n_chunks=n_chunks, conv_width=conv_width),
        grid=(B, n_chunks),
        in_specs=in_specs,
        out_specs=pl.BlockSpec((None, T, C), lambda b, c: (b, cidx(c), 0)),
        out_shape=jax.ShapeDtypeStruct((B, S, C), out_dtype),
        scratch_shapes=scratch,
        compiler_params=_params(("parallel", "arbitrary"), blocks, 8 * _nbytes((T, C), F32)),
        name="rglru_bwd" if reverse else "rglru_fwd",
    )(*args)


def _merge_kernel(o_ref, r_ref, wa_ref, wr_ref, ga_ref, gr_ref, out_ref):
    a = jnp.dot(o_ref[...], wa_ref[...], preferred_element_type=F32)
    r = jnp.dot(r_ref[...], wr_ref[...], preferred_element_type=F32)
    out_ref[...] = (jax.nn.sigmoid(ga_ref[...]) * a + jax.nn.sigmoid(gr_ref[...]) * r).astype(out_ref.dtype)


def _merge(o, rr, wa, wr, rest, gate_col0):
    M, K = o.shape
    N = wa.shape[1]
    bm = _pick(M, 1024, V7X_SUBLANES)
    bn = _pick(N, 512, V7X_LANES)
    nb = N // bn
    ga0 = gate_col0 // bn
    blocks = (2 * _nbytes((bm, K), BF16) + 2 * _nbytes((K, bn), BF16) + 2 * _nbytes((bm, bn), F32)
              + _nbytes((bm, bn), BF16))
    return pl.pallas_call(
        _merge_kernel,
        grid=(M // bm, nb),
        in_specs=[pl.BlockSpec((bm, K), lambda i, j: (i, 0)),
                  pl.BlockSpec((bm, K), lambda i, j: (i, 0)),
                  pl.BlockSpec((K, bn), lambda i, j: (0, j)),
                  pl.BlockSpec((K, bn), lambda i, j: (0, j)),
                  pl.BlockSpec((bm, bn), lambda i, j: (i, ga0 + j)),
                  pl.BlockSpec((bm, bn), lambda i, j: (i, ga0 + nb + j))],
        out_specs=pl.BlockSpec((bm, bn), lambda i, j: (i, j)),
        out_shape=jax.ShapeDtypeStruct((M, N), BF16),
        compiler_params=_params(("parallel", "arbitrary"), blocks, 3 * _nbytes((bm, bn), F32)),
        name="merge",
    )(o, rr, wa, wr, rest, rest)


def _prepare_weights(ffn1_w_in, ffn1_w_out, w_in, conv_w, conv_b, rg_wa, rg_ba, rg_wx, rg_bx, rg_lambda,
                     w_attn_proj, w_rnn_proj, w_out, ffn2_w_in, ffn2_w_out, qk_width, rnn_width):
    def ffn_w(w_i, w_o):
        D, F = w_i.shape[0], w_o.shape[0]
        Fp = -(-F // 1024) * 1024
        pad = Fp - F
        w_gu = jnp.zeros((D, 2 * Fp), BF16)
        w_gu = lax.dynamic_update_slice(w_gu, w_i[:, :F].astype(BF16), (0, 0))
        w_gu = lax.dynamic_update_slice(w_gu, w_i[:, F:].astype(BF16), (0, Fp))
        wo = jnp.pad(w_o.astype(BF16), ((0, pad), (0, 0)))
        return w_gu, wo

    c1, c2, c3 = qk_width, 2 * qk_width, 3 * qk_width
    wi = w_in.astype(BF16)
    nblk, bw = rg_wa.shape[1], rg_wa.shape[2]
    return dict(
        ffn1=ffn_w(ffn1_w_in[0], ffn1_w_out[0]),
        ffn2=ffn_w(ffn2_w_in[0], ffn2_w_out[0]),
        w_in=wi, wq_t=wi[:, :c1].T, wv_t=wi[:, c2:c3].T,
        conv_w=conv_w.astype(F32), conv_b=conv_b.reshape(1, rnn_width).astype(F32),
        wcat=jnp.concatenate([rg_wa, rg_wx], axis=-1).astype(BF16),
        bcat=jnp.concatenate([rg_ba, rg_bx], axis=-1).reshape(2, nblk, 1, 2 * bw).astype(F32),
        lam=rg_lambda.reshape(2, 1, rnn_width).astype(F32),
        w_attn=w_attn_proj.astype(BF16), w_rnn=w_rnn_proj.astype(BF16), w_out=w_out.astype(BF16),
    )


def _trunk(x, p, ffn1_norm, mix_norm, lam_params, subln_gain, ffn2_norm, final_norm, head_dim, rnn_width):
    B, S, D = x.shape
    M = B * S
    x = x.reshape(M, D)
    x = _ffn(x, ffn1_norm, *p["ffn1"])

    h = _rmsnorm(x, mix_norm, BF16)
    h3 = h.reshape(B, S, D)
    cos_t, sin_t, c, s1, s2 = _rope_tables(S, head_dim)
    qt = _proj_t(h3, p["wq_t"], rope=(cos_t, sin_t, head_dim, head_dim ** -0.5 * math.log2(math.e)))
    qk_width = p["wq_t"].shape[0]
    k = _proj_k(h3, p["w_in"], qk_width, qk_width, c, s1, s2, cos_t.shape[0])
    vt = _proj_t(h3, p["wv_t"])
    c3 = 3 * qk_width
    rest = _mm(h, p["w_in"], c3, p["w_in"].shape[1] - c3, F32)

    o = _diff_attention(qt, k, vt, lam_params, subln_gain, head_dim)

    rest3 = rest.reshape(B, S, rest.shape[1])
    rnn = (rest3, 0, 1, p["conv_w"], p["conv_b"])
    hf = _rglru_pass(*rnn, p["wcat"][0], p["bcat"][0], p["lam"][0], None, reverse=False)
    rr = _rglru_pass(*rnn, p["wcat"][1], p["bcat"][1], p["lam"][1], hf, reverse=True)

    merged = _merge(o.reshape(M, -1), rr.reshape(M, rnn_width), p["w_attn"], p["w_rnn"], rest, 2 * rnn_width)
    x = _mm_residual(merged, p["w_out"], x, 1.0)

    x = _ffn(x, ffn2_norm, *p["ffn2"])
    return _rmsnorm(x, final_norm, F32).reshape(B, S, D)


def kernel(x_prompt, x_sample, ffn1_norm, ffn1_w_in, ffn1_w_out, mix_norm, w_in, lambda_q1, lambda_k1, lambda_q2, lambda_k2, subln_gain, conv_w, conv_b, rg_wa, rg_ba, rg_wx, rg_bx, rg_lambda, w_attn_proj, w_rnn_proj, w_out, ffn2_norm, ffn2_w_in, ffn2_w_out, final_norm):
    head_dim = lambda_q1.shape[-1]
    rnn_width = conv_w.shape[-1]
    D = x_prompt.shape[-1]
    qk_width = (w_in.shape[-1] - 2 * rnn_width - 2 * D) // 3
    p = _prepare_weights(ffn1_w_in, ffn1_w_out, w_in[0], conv_w[0], conv_b[0], rg_wa[0], rg_ba[0],
                         rg_wx[0], rg_bx[0], rg_lambda[0], w_attn_proj[0], w_rnn_proj[0], w_out[0],
                         ffn2_w_in, ffn2_w_out, qk_width, rnn_width)
    lam_params = jnp.concatenate([lambda_q1, lambda_k1, lambda_q2, lambda_k2], axis=0).astype(F32)
    args = (p, ffn1_norm[0], mix_norm[0], lam_params, subln_gain[0], ffn2_norm[0], final_norm,
            head_dim, rnn_width)
    return _trunk(x_prompt, *args), _trunk(x_sample, *args)
```

```python
import functools
import math

import jax
import jax.numpy as jnp
from jax import lax
from jax.experimental import pallas as pl
from jax.experimental.pallas import tpu as pltpu

EPS = 1e-6
ROPE_THETA = 500000.0
RG_C = 8.0
LAMBDA_INIT = 0.8 - 0.6 * math.exp(-0.3 * 0)

V7X_LANES = 128
V7X_SUBLANES = 8
V7X_VMEM_BYTES = 64 * 1024 * 1024
VMEM_CAP = V7X_VMEM_BYTES - 6 * 1024 * 1024

F32 = jnp.float32
BF16 = jnp.bfloat16


def _pick(n, target, align):
    best = None
    for d in range(align, min(n, target) + 1, align):
        if n % d == 0:
            best = d
    return best if best is not None else n


def _nbytes(shape, dtype):
    return math.prod(shape) * jnp.dtype(dtype).itemsize


def _params(semantics, pipelined_bytes, resident_bytes=0):
    need = 2 * pipelined_bytes + resident_bytes + 4 * 1024 * 1024
    return pltpu.CompilerParams(dimension_semantics=semantics,
                                vmem_limit_bytes=int(min(max(need, 16 * 1024 * 1024), VMEM_CAP)))


def _rmsnorm_kernel(x_ref, g_ref, o_ref):
    x = x_ref[...]
    ms = jnp.mean(x * x, axis=-1, keepdims=True)
    o_ref[...] = (x * lax.rsqrt(ms + EPS) * g_ref[...]).astype(o_ref.dtype)


def _rmsnorm(x, g, out_dtype):
    M, D = x.shape
    bm = _pick(M, 256, V7X_SUBLANES)
    blocks = _nbytes((bm, D), F32) + _nbytes((bm, D), out_dtype)
    return pl.pallas_call(
        _rmsnorm_kernel,
        grid=(M // bm,),
        in_specs=[pl.BlockSpec((bm, D), lambda i: (i, 0)),
                  pl.BlockSpec((1, D), lambda i: (0, 0))],
        out_specs=pl.BlockSpec((bm, D), lambda i: (i, 0)),
        out_shape=jax.ShapeDtypeStruct((M, D), out_dtype),
        compiler_params=_params(("parallel",), blocks, _nbytes((bm, D), F32)),
        name="rmsnorm",
    )(x, g.reshape(1, D).astype(F32))


def _ffn_up_kernel(h_ref, wg_ref, wu_ref, o_ref):
    h = h_ref[...]
    g = jnp.dot(h, wg_ref[...], preferred_element_type=F32)
    u = jnp.dot(h, wu_ref[...], preferred_element_type=F32)
    o_ref[...] = (g * jax.nn.sigmoid(g) * u).astype(o_ref.dtype)


def _ffn_up(h, w_gu):
    M, D = h.shape
    Fp = w_gu.shape[1] // 2
    bm = _pick(M, 1024, V7X_SUBLANES)
    bf = _pick(Fp, 512, V7X_LANES)
    nf = Fp // bf
    blocks = _nbytes((bm, D), BF16) + 2 * _nbytes((D, bf), BF16) + _nbytes((bm, bf), BF16)
    return pl.pallas_call(
        _ffn_up_kernel,
        grid=(M // bm, nf),
        in_specs=[pl.BlockSpec((bm, D), lambda i, j: (i, 0)),
                  pl.BlockSpec((D, bf), lambda i, j: (0, j)),
                  pl.BlockSpec((D, bf), lambda i, j: (0, nf + j))],
        out_specs=pl.BlockSpec((bm, bf), lambda i, j: (i, j)),
        out_shape=jax.ShapeDtypeStruct((M, Fp), BF16),
        compiler_params=_params(("parallel", "arbitrary"), blocks, 3 * _nbytes((bm, bf), F32)),
        name="ffn_up",
    )(h, w_gu, w_gu)


def _mm_res_kernel(a_ref, w_ref, r_ref, o_ref, *, scale):
    part = scale * jnp.dot(a_ref[...], w_ref[...], preferred_element_type=F32)

    @pl.when(pl.program_id(2) == 0)
    def _():
        o_ref[...] = r_ref[...] + part

    @pl.when(pl.program_id(2) != 0)
    def _():
        o_ref[...] += part


def _mm_residual(a, w, res, scale):
    M, K = a.shape
    N = w.shape[1]
    def footprint(bm, bn, bk):
        return _nbytes((bm, bk), BF16) + _nbytes((bk, bn), BF16) + 2 * _nbytes((bm, bn), F32)

    for t in (1024, 512):
        bm, bn, bk = _pick(M, t, V7X_SUBLANES), _pick(N, t, V7X_LANES), K
        if 2 * footprint(bm, bn, bk) <= VMEM_CAP - 6 * 1024 * 1024:
            break
    else:
        bm, bn = _pick(M, 1024, V7X_SUBLANES), _pick(N, 1024, V7X_LANES)
        bk = _pick(K, 3072, 2 * V7X_LANES)
    blocks = footprint(bm, bn, bk)
    return pl.pallas_call(
        functools.partial(_mm_res_kernel, scale=scale),
        grid=(M // bm, N // bn, K // bk),
        in_specs=[pl.BlockSpec((bm, bk), lambda i, j, k: (i, k)),
                  pl.BlockSpec((bk, bn), lambda i, j, k: (k, j)),
                  pl.BlockSpec((bm, bn), lambda i, j, k: (i, j))],
        out_specs=pl.BlockSpec((bm, bn), lambda i, j, k: (i, j)),
        out_shape=jax.ShapeDtypeStruct((M, N), F32),
        compiler_params=_params(("parallel", "parallel", "arbitrary"), blocks,
                                2 * _nbytes((bm, bn), F32)),
        name="mm_residual",
    )(a, w, res)


def _ffn(x, norm_g, w_gu, wo):
    h = _rmsnorm(x, norm_g, BF16)
    a = _ffn_up(h, w_gu)
    return _mm_residual(a, wo, x, 0.5)


def _mm_kernel(x_ref, w_ref, o_ref):
    o_ref[...] = jnp.dot(x_ref[...], w_ref[...], preferred_element_type=F32).astype(o_ref.dtype)


def _mm(x, w, col0, N, out_dtype):
    M, K = x.shape
    bm = _pick(M, 1024, V7X_SUBLANES)
    bn = _pick(math.gcd(N, col0) if col0 else N, 1024, V7X_LANES)
    j0 = col0 // bn
    blocks = _nbytes((bm, K), BF16) + _nbytes((K, bn), BF16) + _nbytes((bm, bn), out_dtype)
    return pl.pallas_call(
        _mm_kernel,
        grid=(M // bm, N // bn),
        in_specs=[pl.BlockSpec((bm, K), lambda i, j: (i, 0)),
                  pl.BlockSpec((K, bn), lambda i, j: (0, j0 + j))],
        out_specs=pl.BlockSpec((bm, bn), lambda i, j: (i, j)),
        out_shape=jax.ShapeDtypeStruct((M, N), out_dtype),
        compiler_params=_params(("parallel", "arbitrary"), blocks, _nbytes((bm, bn), F32)),
        name="mm",
    )(x, w)


def _nt_dot(w, h):
    return lax.dot_general(w, h, (((1,), (1,)), ((), ())), preferred_element_type=F32)


def _proj_vt_kernel(w_ref, h_ref, o_ref):
    o_ref[...] = _nt_dot(w_ref[...], h_ref[...]).astype(o_ref.dtype)


def _proj_qt_kernel(w_ref, h_ref, cos_ref, sin_ref, o_ref, *, head_dim, rot_half, scale):
    acc = _nt_dot(w_ref[...], h_ref[...])
    cos = cos_ref[...]
    sin = sin_ref[...]
    for g in range(acc.shape[0] // head_dim):
        base = g * head_dim
        x1 = acc[base:base + rot_half]
        x2 = acc[base + rot_half:base + 2 * rot_half]
        rest = acc[base + 2 * rot_half:base + head_dim]
        rot = jnp.concatenate([x1 * cos - x2 * sin, x2 * cos + x1 * sin, rest], axis=0)
        o_ref[base:base + head_dim, :] = (rot * scale).astype(o_ref.dtype)


def _proj_t(h3, wt, rope=None):
    B, S, K = h3.shape
    N = wt.shape[0]
    bm = _pick(S, 1024, V7X_LANES)
    bn = _pick(N, 1024, V7X_LANES)
    blocks = _nbytes((bm, K), BF16) + _nbytes((bn, K), BF16) + _nbytes((bn, bm), BF16)
    in_specs = [pl.BlockSpec((bn, K), lambda b, i, j: (j, 0)),
                pl.BlockSpec((None, bm, K), lambda b, i, j: (b, i, 0))]
    args = [wt, h3]
    if rope is None:
        body = _proj_vt_kernel
    else:
        cos_t, sin_t, head_dim, scale = rope
        rot_half = cos_t.shape[0]
        body = functools.partial(_proj_qt_kernel, head_dim=head_dim, rot_half=rot_half, scale=scale)
        in_specs += [pl.BlockSpec((rot_half, bm), lambda b, i, j: (0, i)),
                     pl.BlockSpec((rot_half, bm), lambda b, i, j: (0, i))]
        args += [cos_t, sin_t]
    return pl.pallas_call(
        body,
        grid=(B, S // bm, N // bn),
        in_specs=in_specs,
        out_specs=pl.BlockSpec((None, bn, bm), lambda b, i, j: (b, j, i)),
        out_shape=jax.ShapeDtypeStruct((B, N, S), BF16),
        compiler_params=_params(("parallel", "parallel", "arbitrary"), blocks,
                                2 * _nbytes((bn, bm), F32)),
        name="proj_t",
    )(*args)


def _proj_k_kernel(h_ref, w_ref, c_ref, s1_ref, s2_ref, o_ref, *, rot_half):
    acc = jnp.dot(h_ref[...], w_ref[...], preferred_element_type=F32)
    c = c_ref[...]
    s1 = s1_ref[...]
    s2 = s2_ref[...]
    for t in range(acc.shape[1] // V7X_LANES):
        x = acc[:, t * V7X_LANES:(t + 1) * V7X_LANES]
        up = pltpu.roll(x, V7X_LANES - rot_half, 1)
        dn = pltpu.roll(x, rot_half, 1)
        o_ref[:, t * V7X_LANES:(t + 1) * V7X_LANES] = (x * c + up * s1 + dn * s2).astype(o_ref.dtype)


def _proj_k(h3, w, col0, N, c, s1, s2, rot_half):
    B, S, K = h3.shape
    bm = _pick(S, 1024, V7X_SUBLANES)
    bn = _pick(math.gcd(N, col0) if col0 else N, 1024, V7X_LANES)
    j0 = col0 // bn
    blocks = (_nbytes((bm, K), BF16) + _nbytes((K, bn), BF16) + _nbytes((bm, bn), BF16)
              + 3 * _nbytes((bm, V7X_LANES), F32))
    tab = pl.BlockSpec((bm, V7X_LANES), lambda b, i, j: (i, 0))
    return pl.pallas_call(
        functools.partial(_proj_k_kernel, rot_half=rot_half),
        grid=(B, S // bm, N // bn),
        in_specs=[pl.BlockSpec((None, bm, K), lambda b, i, j: (b, i, 0)),
                  pl.BlockSpec((K, bn), lambda b, i, j: (0, j0 + j)),
                  tab, tab, tab],
        out_specs=pl.BlockSpec((None, bm, bn), lambda b, i, j: (b, i, j)),
        out_shape=jax.ShapeDtypeStruct((B, S, N), BF16),
        compiler_params=_params(("parallel", "parallel", "arbitrary"), blocks,
                                2 * _nbytes((bm, bn), F32)),
        name="proj_k",
    )(h3, w, c, s1, s2)


def _rope_tables(S, head_dim):
    rot_dim = head_dim // 4
    rot_half = rot_dim // 2
    inv_freq = ROPE_THETA ** (-jnp.arange(0, rot_dim, 2, dtype=F32) / rot_dim)
    ang = jnp.arange(S).astype(F32)[:, None] * inv_freq[None, :]
    cos, sin = jnp.cos(ang), jnp.sin(ang)
    d = jnp.arange(V7X_LANES) % head_dim
    idx = d % rot_half
    lo = d < rot_half
    hi = (d >= rot_half) & (d < rot_dim)
    c = jnp.where((lo | hi)[None, :], cos[:, idx], 1.0)
    s1 = jnp.where(lo[None, :], -sin[:, idx], 0.0)
    s2 = jnp.where(hi[None, :], sin[:, idx], 0.0)
    return cos.T, sin.T, c, s1, s2


def _attn_kernel(lam_ref, qt_ref, k_ref, vt_ref, gain_ref, o_ref,
                 qbd_ref, vte_ref, sa_ref, sb_ref, acc1_ref, acc2_ref, *, tk, head_dim):
    tq = qt_ref.shape[1]
    S = k_ref.shape[0]
    hw = 2 * head_dim
    n_chunks = S // tk

    @pl.when(pl.program_id(2) == 0)
    def _():
        vte_ref[0:hw, :] = vt_ref[...]
        vte_ref[hw:, :] = jnp.ones((vte_ref.shape[0] - hw, S), vte_ref.dtype)

    qbd_ref[...] = jnp.zeros_like(qbd_ref)
    qbd_ref[0:head_dim, 0:tq] = qt_ref[0:head_dim, :]
    qbd_ref[head_dim:hw, tq:2 * tq] = qt_ref[head_dim:hw, :]
    acc1_ref[...] = jnp.zeros_like(acc1_ref)
    acc2_ref[...] = jnp.zeros_like(acc2_ref)

    def scores(c, s_ref):
        start = pl.multiple_of(c * tk, tk)
        s = jnp.dot(k_ref[pl.ds(start, tk), :], qbd_ref[...], preferred_element_type=F32)
        s_ref[...] = s
        return jnp.max(s, axis=0, keepdims=True)

    def accumulate(c, s_ref, cmax, m1, m2):
        start = pl.multiple_of(c * tk, tk)
        v = vte_ref[:, pl.ds(start, tk)]
        ms = []
        for m, acc_ref, cols in ((m1, acc1_ref, slice(0, tq)), (m2, acc2_ref, slice(tq, 2 * tq))):
            m_new = jnp.maximum(m, cmax[:, cols])
            alpha = jnp.exp2(m - m_new)
            p = jnp.exp2(s_ref[:, cols] - m_new).astype(BF16)
            acc_ref[...] = alpha * acc_ref[...] + jnp.dot(v, p, preferred_element_type=F32)
            ms.append(m_new)
        return ms

    per_trip = next(u for u in (8, 4, 2) if n_chunks % u == 0)
    bufs = (sa_ref, sb_ref)

    def body(j, carry):
        m1, m2, cmax = carry
        c = per_trip * j
        for i in range(per_trip):
            cmax_next = scores(jnp.minimum(c + i + 1, n_chunks - 1), bufs[(i + 1) % 2])
            m1, m2 = accumulate(c + i, bufs[i % 2], cmax, m1, m2)
            cmax = cmax_next
        return m1, m2, cmax

    neg = jnp.full((1, tq), -jnp.inf, F32)
    lax.fori_loop(0, n_chunks // per_trip, body, (neg, neg, scores(0, sa_ref)))

    lq1, lk1, lq2, lk2 = (lam_ref[i:i + 1, :] for i in range(4))
    lam = (jnp.exp(jnp.sum(lq1 * lk1, axis=-1, keepdims=True))
           - jnp.exp(jnp.sum(lq2 * lk2, axis=-1, keepdims=True)) + LAMBDA_INIT)
    o = (acc1_ref[0:hw, :] / acc1_ref[hw:hw + 1, :]
         - lam * (acc2_ref[0:hw, :] / acc2_ref[hw:hw + 1, :]))
    ms = jnp.mean(o * o, axis=0, keepdims=True)
    o = o * lax.rsqrt(ms + EPS) * gain_ref[...] * (1.0 - LAMBDA_INIT)
    o_ref[...] = o.T.astype(o_ref.dtype)


ONES_ROWS = 2 * V7X_SUBLANES


def _diff_attention(qt, k, vt, lam_params, gain, head_dim):
    B, W, S = qt.shape
    hw = 2 * head_dim
    H = W // hw
    tq = _pick(S, 512, V7X_LANES)
    tk = _pick(S // 2, 512, V7X_LANES)
    blocks = (_nbytes((hw, tq), BF16) + 2 * _nbytes((S, hw), BF16) + _nbytes((tq, hw), BF16))
    scratch = (_nbytes((hw, 2 * tq), BF16) + _nbytes((hw + ONES_ROWS, S), BF16)
               + 2 * _nbytes((tk, 2 * tq), F32) + 2 * _nbytes((hw + ONES_ROWS, tq), F32))
    temps = 2 * _nbytes((tk, 2 * tq), F32)
    return pl.pallas_call(
        functools.partial(_attn_kernel, tk=tk, head_dim=head_dim),
        grid=(B, H, S // tq),
        in_specs=[pl.BlockSpec((4, head_dim), lambda b, h, i: (0, 0)),
                  pl.BlockSpec((None, hw, tq), lambda b, h, i: (b, h, i)),
                  pl.BlockSpec((None, S, hw), lambda b, h, i: (b, 0, h)),
                  pl.BlockSpec((None, hw, S), lambda b, h, i: (b, h, 0)),
                  pl.BlockSpec((hw, 1), lambda b, h, i: (0, 0))],
        out_specs=pl.BlockSpec((None, tq, hw), lambda b, h, i: (b, i, h)),
        out_shape=jax.ShapeDtypeStruct((B, S, W), BF16),
        scratch_shapes=[pltpu.VMEM((hw, 2 * tq), BF16),
                        pltpu.VMEM((hw + ONES_ROWS, S), BF16),
                        pltpu.VMEM((tk, 2 * tq), F32),
                        pltpu.VMEM((tk, 2 * tq), F32),
                        pltpu.VMEM((hw + ONES_ROWS, tq), F32),
                        pltpu.VMEM((hw + ONES_ROWS, tq), F32)],
        compiler_params=_params(("arbitrary", "arbitrary", "arbitrary"), blocks, scratch + temps),
        name="diff_attention",
    )(lam_params, qt, k, vt, gain.reshape(hw, 1).astype(F32))


def _softplus(x):
    return jnp.maximum(x, 0.0) + jnp.log1p(jnp.exp(-jnp.abs(x)))


def _rglru_kernel(*refs, reverse, n_chunks, conv_width):
    if reverse:
        (x_ref, xp_ref, xn_ref, cw_ref, cb_ref, w_ref, b_ref, lam_ref, hf_ref, y_ref,
         o_ref, xe_ref, a_ref, u_ref, h_ref) = refs
    else:
        (x_ref, xp_ref, xn_ref, cw_ref, cb_ref, w_ref, b_ref, lam_ref,
         o_ref, xe_ref, a_ref, u_ref, h_ref) = refs
    T = x_ref.shape[0]
    halo = xp_ref.shape[0]
    nblk, bw = w_ref.shape[0], w_ref.shape[1]
    step = pl.program_id(1)
    chunk = (n_chunks - 1 - step) if reverse else step

    xe_ref[0:halo, :] = jnp.where(chunk == 0, 0.0, xp_ref[...])
    xe_ref[halo:halo + T, :] = x_ref[...]
    xe_ref[halo + T:2 * halo + T, :] = jnp.where(chunk == n_chunks - 1, 0.0, xn_ref[...])
    left = conv_width // 2
    xc = cb_ref[...] + cw_ref[0:1, :] * xe_ref[halo - left:halo - left + T, :]
    for j in range(1, conv_width):
        xc = xc + cw_ref[j:j + 1, :] * xe_ref[halo - left + j:halo - left + j + T, :]

    sp = _softplus(-lam_ref[...])
    for n in range(nblk):
        cols = slice(n * bw, (n + 1) * bw)
        xb = xc[:, cols]
        z = jnp.dot(xb.astype(BF16), w_ref[n], preferred_element_type=F32) + b_ref[n]
        r = jax.nn.sigmoid(z[:, :bw])
        i = jax.nn.sigmoid(z[:, bw:])
        log_a = -RG_C * r * sp[:, cols]
        a_ref[:, cols] = jnp.exp(log_a)
        g = jnp.maximum(1.0 - jnp.exp(2.0 * log_a), 0.0)
        gain = g * lax.rsqrt(jnp.maximum(g, jnp.finfo(F32).tiny))
        u_ref[:, cols] = gain * (i * xb)

    @pl.when(step == 0)
    def _():
        h_ref[...] = jnp.zeros_like(h_ref)

    def group(g, h):
        gi = (T // V7X_SUBLANES - 1 - g) if reverse else g
        base = pl.multiple_of(gi * V7X_SUBLANES, V7X_SUBLANES)
        order = range(V7X_SUBLANES - 1, -1, -1) if reverse else range(V7X_SUBLANES)
        for r_ in order:
            row = pl.ds(base + r_, 1)
            h = a_ref[row, :] * h + u_ref[row, :]
            u_ref[row, :] = h
        return h

    h_ref[...] = lax.fori_loop(0, T // V7X_SUBLANES, group, h_ref[...])

    if reverse:
        o_ref[...] = ((hf_ref[...] + u_ref[...]) * jax.nn.gelu(y_ref[...])).astype(o_ref.dtype)
    else:
        o_ref[...] = u_ref[...]


def _rglru_pass(rest, x_col, y_col, conv_w, conv_b, wcat, bcat, lam, hf, reverse):
    B, S, _ = rest.shape
    nblk, bw, _ = wcat.shape
    C = nblk * bw
    conv_width = conv_w.shape[0]
    T = _pick(S, 256, V7X_SUBLANES)
    halo = V7X_SUBLANES
    n_chunks = S // T
    hpc = T // halo

    def cidx(c):
        return (n_chunks - 1 - c) if reverse else c

    in_specs = [
        pl.BlockSpec((None, T, C), lambda b, c: (b, cidx(c), x_col)),
        pl.BlockSpec((None, halo, C),
                     lambda b, c: (b, jnp.maximum(cidx(c) * hpc - 1, 0), x_col)),
        pl.BlockSpec((None, halo, C),
                     lambda b, c: (b, jnp.minimum((cidx(c) + 1) * hpc, S // halo - 1), x_col)),
        pl.BlockSpec((conv_width, C), lambda b, c: (0, 0)),
        pl.BlockSpec((1, C), lambda b, c: (0, 0)),
        pl.BlockSpec((nblk, bw, 2 * bw), lambda b, c: (0, 0, 0)),
        pl.BlockSpec((nblk, 1, 2 * bw), lambda b, c: (0, 0, 0)),
        pl.BlockSpec((1, C), lambda b, c: (0, 0)),
    ]
    args = [rest, rest, rest, conv_w, conv_b, wcat, bcat, lam]
    blocks = 2 * _nbytes((T, C), F32) + 2 * _nbytes((halo, C), F32) + 2 * _nbytes(wcat.shape, BF16)
    if reverse:
        in_specs += [pl.BlockSpec((None, T, C), lambda b, c: (b, cidx(c), 0)),
                     pl.BlockSpec((None, T, C), lambda b, c: (b, cidx(c), y_col))]
        args += [hf, rest]
        blocks += 2 * _nbytes((T, C), F32)
        out_dtype = BF16
    else:
        out_dtype = F32
    scratch = [pltpu.VMEM((T + 2 * halo, C), F32), pltpu.VMEM((T, C), F32),
               pltpu.VMEM((T, C), F32), pltpu.VMEM((1, C), F32)]
    return pl.pallas_call(
        functools.partial(_rglru_kernel, reverse=reverse, n_chunks=n_chunks, conv_width=conv_width),
        grid=(B, n_chunks),
        in_specs=in_specs,
        out_specs=pl.BlockSpec((None, T, C), lambda b, c: (b, cidx(c), 0)),
        out_shape=jax.ShapeDtypeStruct((B, S, C), out_dtype),
        scratch_shapes=scratch,
        compiler_params=_params(("parallel", "arbitrary"), blocks, 8 * _nbytes((T, C), F32)),
        name="rglru_bwd" if reverse else "rglru_fwd",
    )(*args)


def _merge_kernel(o_ref, r_ref, wa_ref, wr_ref, ga_ref, gr_ref, out_ref):
    a = jnp.dot(o_ref[...], wa_ref[...], preferred_element_type=F32)
    r = jnp.dot(r_ref[...], wr_ref[...], preferred_element_type=F32)
    out_ref[...] = (jax.nn.sigmoid(ga_ref[...]) * a + jax.nn.sigmoid(gr_ref[...]) * r).astype(out_ref.dtype)


def _merge(o, rr, wa, wr, rest, gate_col0):
    M, K = o.shape
    N = wa.shape[1]
    bm = _pick(M, 1024, V7X_SUBLANES)
    bn = _pick(N, 512, V7X_LANES)
    nb = N // bn
    ga0 = gate_col0 // bn
    blocks = (2 * _nbytes((bm, K), BF16) + 2 * _nbytes((K, bn), BF16) + 2 * _nbytes((bm, bn), F32)
              + _nbytes((bm, bn), BF16))
    return pl.pallas_call(
        _merge_kernel,
        grid=(M // bm, nb),
        in_specs=[pl.BlockSpec((bm, K), lambda i, j: (i, 0)),
                  pl.BlockSpec((bm, K), lambda i, j: (i, 0)),
                  pl.BlockSpec((K, bn), lambda i, j: (0, j)),
                  pl.BlockSpec((K, bn), lambda i, j: (0, j)),
                  pl.BlockSpec((bm, bn), lambda i, j: (i, ga0 + j)),
                  pl.BlockSpec((bm, bn), lambda i, j: (i, ga0 + nb + j))],
        out_specs=pl.BlockSpec((bm, bn), lambda i, j: (i, j)),
        out_shape=jax.ShapeDtypeStruct((M, N), BF16),
        compiler_params=_params(("parallel", "arbitrary"), blocks, 3 * _nbytes((bm, bn), F32)),
        name="merge",
    )(o, rr, wa, wr, rest, rest)


def _prepare_weights(ffn1_w_in, ffn1_w_out, w_in, conv_w, conv_b, rg_wa, rg_ba, rg_wx, rg_bx, rg_lambda,
                     w_attn_proj, w_rnn_proj, w_out, ffn2_w_in, ffn2_w_out, qk_width, rnn_width):
    def ffn_w(w_i, w_o):
        D, F = w_i.shape[0], w_o.shape[0]
        Fp = -(-F // 1024) * 1024
        pad = Fp - F
        w_gu = jnp.zeros((D, 2 * Fp), BF16)
        w_gu = lax.dynamic_update_slice(w_gu, w_i[:, :F].astype(BF16), (0, 0))
        w_gu = lax.dynamic_update_slice(w_gu, w_i[:, F:].astype(BF16), (0, Fp))
        wo = jnp.pad(w_o.astype(BF16), ((0, pad), (0, 0)))
        return w_gu, wo

    c1, c2, c3 = qk_width, 2 * qk_width, 3 * qk_width
    wi = w_in.astype(BF16)
    nblk, bw = rg_wa.shape[1], rg_wa.shape[2]
    return dict(
        ffn1=ffn_w(ffn1_w_in[0], ffn1_w_out[0]),
        ffn2=ffn_w(ffn2_w_in[0], ffn2_w_out[0]),
        w_in=wi, wq_t=wi[:, :c1].T, wv_t=wi[:, c2:c3].T,
        conv_w=conv_w.astype(F32), conv_b=conv_b.reshape(1, rnn_width).astype(F32),
        wcat=jnp.concatenate([rg_wa, rg_wx], axis=-1).astype(BF16),
        bcat=jnp.concatenate([rg_ba, rg_bx], axis=-1).reshape(2, nblk, 1, 2 * bw).astype(F32),
        lam=rg_lambda.reshape(2, 1, rnn_width).astype(F32),
        w_attn=w_attn_proj.astype(BF16), w_rnn=w_rnn_proj.astype(BF16), w_out=w_out.astype(BF16),
    )


def _trunk(x, p, ffn1_norm, mix_norm, lam_params, subln_gain, ffn2_norm, final_norm, head_dim, rnn_width):
    B, S, D = x.shape
    M = B * S
    x = x.reshape(M, D)
    x = _ffn(x, ffn1_norm, *p["ffn1"])

    h = _rmsnorm(x, mix_norm, BF16)
    h3 = h.reshape(B, S, D)
    cos_t, sin_t, c, s1, s2 = _rope_tables(S, head_dim)
    qt = _proj_t(h3, p["wq_t"], rope=(cos_t, sin_t, head_dim, head_dim ** -0.5 * math.log2(math.e)))
    qk_width = p["wq_t"].shape[0]
    k = _proj_k(h3, p["w_in"], qk_width, qk_width, c, s1, s2, cos_t.shape[0])
    vt = _proj_t(h3, p["wv_t"])
    c3 = 3 * qk_width
    rest = _mm(h, p["w_in"], c3, p["w_in"].shape[1] - c3, F32)

    o = _diff_attention(qt, k, vt, lam_params, subln_gain, head_dim)

    rest3 = rest.reshape(B, S, rest.shape[1])
    rnn = (rest3, 0, 1, p["conv_w"], p["conv_b"])
    hf = _rglru_pass(*rnn, p["wcat"][0], p["bcat"][0], p["lam"][0], None, reverse=False)
    rr = _rglru_pass(*rnn, p["wcat"][1], p["bcat"][1], p["lam"][1], hf, reverse=True)

    merged = _merge(o.reshape(M, -1), rr.reshape(M, rnn_width), p["w_attn"], p["w_rnn"], rest, 2 * rnn_width)
    x = _mm_residual(merged, p["w_out"], x, 1.0)

    x = _ffn(x, ffn2_norm, *p["ffn2"])
    return _rmsnorm(x, final_norm, F32).reshape(B, S, D)


def kernel(x_prompt, x_sample, ffn1_norm, ffn1_w_in, ffn1_w_out, mix_norm, w_in, lambda_q1, lambda_k1, lambda_q2, lambda_k2, subln_gain, conv_w, conv_b, rg_wa, rg_ba, rg_wx, rg_bx, rg_lambda, w_attn_proj, w_rnn_proj, w_out, ffn2_norm, ffn2_w_in, ffn2_w_out, final_norm):
    head_dim = lambda_q1.shape[-1]
    rnn_width = conv_w.shape[-1]
    D = x_prompt.shape[-1]
    qk_width = (w_in.shape[-1] - 2 * rnn_width - 2 * D) // 3
    p = _prepare_weights(ffn1_w_in, ffn1_w_out, w_in[0], conv_w[0], conv_b[0], rg_wa[0], rg_ba[0],
                         rg_wx[0], rg_bx[0], rg_lambda[0], w_attn_proj[0], w_rnn_proj[0], w_out[0],
                         ffn2_w_in, ffn2_w_out, qk_width, rnn_width)
    lam_params = jnp.concatenate([lambda_q1, lambda_k1, lambda_q2, lambda_k2], axis=0).astype(F32)
    args = (p, ffn1_norm[0], mix_norm[0], lam_params, subln_gain[0], ffn2_norm[0], final_norm,
            head_dim, rnn_width)
    return _trunk(x_prompt, *args), _trunk(x_sample, *args)
```

```python
import functools
import math

import jax
import jax.numpy as jnp
from jax import lax
from jax.experimental import pallas as pl
from jax.experimental.pallas import tpu as pltpu

EPS = 1e-6
ROPE_THETA = 500000.0
RG_C = 8.0
LAMBDA_INIT = 0.8 - 0.6 * math.exp(-0.3 * 0)

V7X_LANES = 128
V7X_SUBLANES = 8
V7X_VMEM_BYTES = 64 * 1024 * 1024
VMEM_CAP = V7X_VMEM_BYTES - 6 * 1024 * 1024

F32 = jnp.float32
BF16 = jnp.bfloat16


def _pick(n, target, align):
    best = None
    for d in range(align, min(n, target) + 1, align):
        if n % d == 0:
            best = d
    return best if best is not None else n


def _nbytes(shape, dtype):
    return math.prod(shape) * jnp.dtype(dtype).itemsize


def _params(semantics, pipelined_bytes, resident_bytes=0):
    need = 2 * pipelined_bytes + resident_bytes + 4 * 1024 * 1024
    return pltpu.CompilerParams(dimension_semantics=semantics,
                                vmem_limit_bytes=int(min(max(need, 16 * 1024 * 1024), VMEM_CAP)))


def _rmsnorm_kernel(x_ref, g_ref, o_ref):
    x = x_ref[...]
    ms = jnp.mean(x * x, axis=-1, keepdims=True)
    o_ref[...] = (x * lax.rsqrt(ms + EPS) * g_ref[...]).astype(o_ref.dtype)


def _rmsnorm(x, g, out_dtype):
    M, D = x.shape
    bm = _pick(M, 256, V7X_SUBLANES)
    blocks = _nbytes((bm, D), F32) + _nbytes((bm, D), out_dtype)
    return pl.pallas_call(
        _rmsnorm_kernel,
        grid=(M // bm,),
        in_specs=[pl.BlockSpec((bm, D), lambda i: (i, 0)),
                  pl.BlockSpec((1, D), lambda i: (0, 0))],
        out_specs=pl.BlockSpec((bm, D), lambda i: (i, 0)),
        out_shape=jax.ShapeDtypeStruct((M, D), out_dtype),
        compiler_params=_params(("parallel",), blocks, _nbytes((bm, D), F32)),
        name="rmsnorm",
    )(x, g.reshape(1, D).astype(F32))


def _ffn_up_kernel(h_ref, wg_ref, wu_ref, o_ref):
    h = h_ref[...]
    g = jnp.dot(h, wg_ref[...], preferred_element_type=F32)
    u = jnp.dot(h, wu_ref[...], preferred_element_type=F32)
    o_ref[...] = (g * jax.nn.sigmoid(g) * u).astype(o_ref.dtype)


def _ffn_up(h, w_gu):
    M, D = h.shape
    Fp = w_gu.shape[1] // 2
    bf = _pick(Fp, 512, V7X_LANES)
    bm = _pick(M, 1024 * 512 // bf, V7X_SUBLANES)
    nf = Fp // bf
    blocks = _nbytes((bm, D), BF16) + 2 * _nbytes((D, bf), BF16) + _nbytes((bm, bf), BF16)
    return pl.pallas_call(
        _ffn_up_kernel,
        grid=(M // bm, nf),
        in_specs=[pl.BlockSpec((bm, D), lambda i, j: (i, 0)),
                  pl.BlockSpec((D, bf), lambda i, j: (0, j)),
                  pl.BlockSpec((D, bf), lambda i, j: (0, nf + j))],
        out_specs=pl.BlockSpec((bm, bf), lambda i, j: (i, j)),
        out_shape=jax.ShapeDtypeStruct((M, Fp), BF16),
        compiler_params=_params(("parallel", "arbitrary"), blocks, 3 * _nbytes((bm, bf), F32)),
        name="ffn_up",
    )(h, w_gu, w_gu)


def _mm_res_kernel(a_ref, w_ref, r_ref, o_ref, *, scale):
    part = scale * jnp.dot(a_ref[...], w_ref[...], preferred_element_type=F32)

    @pl.when(pl.program_id(2) == 0)
    def _():
        o_ref[...] = r_ref[...] + part

    @pl.when(pl.program_id(2) != 0)
    def _():
        o_ref[...] += part


def _mm_residual(a, w, res, scale):
    M, K = a.shape
    N = w.shape[1]
    def footprint(bm, bn, bk):
        return _nbytes((bm, bk), BF16) + _nbytes((bk, bn), BF16) + 2 * _nbytes((bm, bn), F32)

    for t in (1024, 512):
        bm, bn, bk = _pick(M, t, V7X_SUBLANES), _pick(N, t, V7X_LANES), K
        if 2 * footprint(bm, bn, bk) <= VMEM_CAP - 6 * 1024 * 1024:
            break
    else:
        bm, bn = _pick(M, 1024, V7X_SUBLANES), _pick(N, 1024, V7X_LANES)
        bk = _pick(K, 3072, 2 * V7X_LANES)
    blocks = footprint(bm, bn, bk)
    return pl.pallas_call(
        functools.partial(_mm_res_kernel, scale=scale),
        grid=(M // bm, N // bn, K // bk),
        in_specs=[pl.BlockSpec((bm, bk), lambda i, j, k: (i, k)),
                  pl.BlockSpec((bk, bn), lambda i, j, k: (k, j)),
                  pl.BlockSpec((bm, bn), lambda i, j, k: (i, j))],
        out_specs=pl.BlockSpec((bm, bn), lambda i, j, k: (i, j)),
        out_shape=jax.ShapeDtypeStruct((M, N), F32),
        compiler_params=_params(("parallel", "parallel", "arbitrary"), blocks,
                                2 * _nbytes((bm, bn), F32)),
        name="mm_residual",
    )(a, w, res)


def _ffn(x, norm_g, w_gu, wo):
    h = _rmsnorm(x, norm_g, BF16)
    a = _ffn_up(h, w_gu)
    return _mm_residual(a, wo, x, 0.5)


def _mm_kernel(x_ref, w_ref, o_ref):
    o_ref[...] = jnp.dot(x_ref[...], w_ref[...], preferred_element_type=F32).astype(o_ref.dtype)


def _mm(x, w, col0, N, out_dtype):
    M, K = x.shape
    bm = _pick(M, 1024, V7X_SUBLANES)
    bn = _pick(math.gcd(N, col0) if col0 else N, 1024, V7X_LANES)
    j0 = col0 // bn
    blocks = _nbytes((bm, K), BF16) + _nbytes((K, bn), BF16) + _nbytes((bm, bn), out_dtype)
    return pl.pallas_call(
        _mm_kernel,
        grid=(M // bm, N // bn),
        in_specs=[pl.BlockSpec((bm, K), lambda i, j: (i, 0)),
                  pl.BlockSpec((K, bn), lambda i, j: (0, j0 + j))],
        out_specs=pl.BlockSpec((bm, bn), lambda i, j: (i, j)),
        out_shape=jax.ShapeDtypeStruct((M, N), out_dtype),
        compiler_params=_params(("parallel", "arbitrary"), blocks, _nbytes((bm, bn), F32)),
        name="mm",
    )(x, w)


def _nt_dot(w, h):
    return lax.dot_general(w, h, (((1,), (1,)), ((), ())), preferred_element_type=F32)


def _proj_vt_kernel(w_ref, h_ref, o_ref):
    o_ref[...] = _nt_dot(w_ref[...], h_ref[...]).astype(o_ref.dtype)


def _proj_qt_kernel(w_ref, h_ref, cos_ref, sin_ref, o_ref, *, head_dim, rot_half, scale):
    acc = _nt_dot(w_ref[...], h_ref[...])
    cos = cos_ref[...]
    sin = sin_ref[...]
    for g in range(acc.shape[0] // head_dim):
        base = g * head_dim
        x1 = acc[base:base + rot_half]
        x2 = acc[base + rot_half:base + 2 * rot_half]
        rest = acc[base + 2 * rot_half:base + head_dim]
        rot = jnp.concatenate([x1 * cos - x2 * sin, x2 * cos + x1 * sin, rest], axis=0)
        o_ref[base:base + head_dim, :] = (rot * scale).astype(o_ref.dtype)


def _proj_t(h3, wt, rope=None):
    B, S, K = h3.shape
    N = wt.shape[0]
    bm = _pick(S, 1024, V7X_LANES)
    bn = _pick(N, 1024, V7X_LANES)
    blocks = _nbytes((bm, K), BF16) + _nbytes((bn, K), BF16) + _nbytes((bn, bm), BF16)
    in_specs = [pl.BlockSpec((bn, K), lambda b, i, j: (j, 0)),
                pl.BlockSpec((None, bm, K), lambda b, i, j: (b, i, 0))]
    args = [wt, h3]
    if rope is None:
        body = _proj_vt_kernel
    else:
        cos_t, sin_t, head_dim, scale = rope
        rot_half = cos_t.shape[0]
        body = functools.partial(_proj_qt_kernel, head_dim=head_dim, rot_half=rot_half, scale=scale)
        in_specs += [pl.BlockSpec((rot_half, bm), lambda b, i, j: (0, i)),
                     pl.BlockSpec((rot_half, bm), lambda b, i, j: (0, i))]
        args += [cos_t, sin_t]
    return pl.pallas_call(
        body,
        grid=(B, S // bm, N // bn),
        in_specs=in_specs,
        out_specs=pl.BlockSpec((None, bn, bm), lambda b, i, j: (b, j, i)),
        out_shape=jax.ShapeDtypeStruct((B, N, S), BF16),
        compiler_params=_params(("parallel", "parallel", "arbitrary"), blocks,
                                2 * _nbytes((bn, bm), F32)),
        name="proj_t",
    )(*args)


def _proj_k_kernel(h_ref, w_ref, c_ref, s1_ref, s2_ref, o_ref, *, rot_half):
    acc = jnp.dot(h_ref[...], w_ref[...], preferred_element_type=F32)
    c = c_ref[...]
    s1 = s1_ref[...]
    s2 = s2_ref[...]
    for t in range(acc.shape[1] // V7X_LANES):
        x = acc[:, t * V7X_LANES:(t + 1) * V7X_LANES]
        up = pltpu.roll(x, V7X_LANES - rot_half, 1)
        dn = pltpu.roll(x, rot_half, 1)
        o_ref[:, t * V7X_LANES:(t + 1) * V7X_LANES] = (x * c + up * s1 + dn * s2).astype(o_ref.dtype)


def _proj_k(h3, w, col0, N, c, s1, s2, rot_half):
    B, S, K = h3.shape
    bm = _pick(S, 1024, V7X_SUBLANES)
    bn = _pick(math.gcd(N, col0) if col0 else N, 1024, V7X_LANES)
    j0 = col0 // bn
    blocks = (_nbytes((bm, K), BF16) + _nbytes((K, bn), BF16) + _nbytes((bm, bn), BF16)
              + 3 * _nbytes((bm, V7X_LANES), F32))
    tab = pl.BlockSpec((bm, V7X_LANES), lambda b, i, j: (i, 0))
    return pl.pallas_call(
        functools.partial(_proj_k_kernel, rot_half=rot_half),
        grid=(B, S // bm, N // bn),
        in_specs=[pl.BlockSpec((None, bm, K), lambda b, i, j: (b, i, 0)),
                  pl.BlockSpec((K, bn), lambda b, i, j: (0, j0 + j)),
                  tab, tab, tab],
        out_specs=pl.BlockSpec((None, bm, bn), lambda b, i, j: (b, i, j)),
        out_shape=jax.ShapeDtypeStruct((B, S, N), BF16),
        compiler_params=_params(("parallel", "parallel", "arbitrary"), blocks,
                                2 * _nbytes((bm, bn), F32)),
        name="proj_k",
    )(h3, w, c, s1, s2)


def _rope_tables(S, head_dim):
    rot_dim = head_dim // 4
    rot_half = rot_dim // 2
    inv_freq = ROPE_THETA ** (-jnp.arange(0, rot_dim, 2, dtype=F32) / rot_dim)
    ang = jnp.arange(S).astype(F32)[:, None] * inv_freq[None, :]
    cos, sin = jnp.cos(ang), jnp.sin(ang)
    d = jnp.arange(V7X_LANES) % head_dim
    idx = d % rot_half
    lo = d < rot_half
    hi = (d >= rot_half) & (d < rot_dim)
    c = jnp.where((lo | hi)[None, :], cos[:, idx], 1.0)
    s1 = jnp.where(lo[None, :], -sin[:, idx], 0.0)
    s2 = jnp.where(hi[None, :], sin[:, idx], 0.0)
    return cos.T, sin.T, c, s1, s2


def _attn_kernel(lam_ref, qt_ref, k_ref, vt_ref, gain_ref, o_ref,
                 qbd_ref, vte_ref, sa_ref, sb_ref, acc1_ref, acc2_ref, *, tk, head_dim):
    tq = qt_ref.shape[1]
    S = k_ref.shape[0]
    hw = 2 * head_dim
    n_chunks = S // tk

    @pl.when(pl.program_id(2) == 0)
    def _():
        vte_ref[0:hw, :] = vt_ref[...]
        vte_ref[hw:, :] = jnp.ones((vte_ref.shape[0] - hw, S), vte_ref.dtype)

    qbd_ref[...] = jnp.zeros_like(qbd_ref)
    qbd_ref[0:head_dim, 0:tq] = qt_ref[0:head_dim, :]
    qbd_ref[head_dim:hw, tq:2 * tq] = qt_ref[head_dim:hw, :]
    acc1_ref[...] = jnp.zeros_like(acc1_ref)
    acc2_ref[...] = jnp.zeros_like(acc2_ref)

    def scores(c, s_ref):
        start = pl.multiple_of(c * tk, tk)
        s = jnp.dot(k_ref[pl.ds(start, tk), :], qbd_ref[...], preferred_element_type=F32)
        s_ref[...] = s
        return jnp.max(s, axis=0, keepdims=True)

    def accumulate(c, s_ref, cmax, m1, m2):
        start = pl.multiple_of(c * tk, tk)
        v = vte_ref[:, pl.ds(start, tk)]
        ms = []
        for m, acc_ref, cols in ((m1, acc1_ref, slice(0, tq)), (m2, acc2_ref, slice(tq, 2 * tq))):
            m_new = jnp.maximum(m, cmax[:, cols])
            alpha = jnp.exp2(m - m_new)
            p = jnp.exp2(s_ref[:, cols] - m_new).astype(BF16)
            acc_ref[...] = alpha * acc_ref[...] + jnp.dot(v, p, preferred_element_type=F32)
            ms.append(m_new)
        return ms

    per_trip = next(u for u in (8, 4, 2) if n_chunks % u == 0)
    bufs = (sa_ref, sb_ref)

    def body(j, carry):
        m1, m2, cmax = carry
        c = per_trip * j
        for i in range(per_trip):
            cmax_next = scores(jnp.minimum(c + i + 1, n_chunks - 1), bufs[(i + 1) % 2])
            m1, m2 = accumulate(c + i, bufs[i % 2], cmax, m1, m2)
            cmax = cmax_next
        return m1, m2, cmax

    neg = jnp.full((1, tq), -jnp.inf, F32)
    lax.fori_loop(0, n_chunks // per_trip, body, (neg, neg, scores(0, sa_ref)))

    lq1, lk1, lq2, lk2 = (lam_ref[i:i + 1, :] for i in range(4))
    lam = (jnp.exp(jnp.sum(lq1 * lk1, axis=-1, keepdims=True))
           - jnp.exp(jnp.sum(lq2 * lk2, axis=-1, keepdims=True)) + LAMBDA_INIT)
    o = (acc1_ref[0:hw, :] / acc1_ref[hw:hw + 1, :]
         - lam * (acc2_ref[0:hw, :] / acc2_ref[hw:hw + 1, :]))
    ms = jnp.mean(o * o, axis=0, keepdims=True)
    o = o * lax.rsqrt(ms + EPS) * gain_ref[...] * (1.0 - LAMBDA_INIT)
    o_ref[...] = o.T.astype(o_ref.dtype)


ONES_ROWS = 2 * V7X_SUBLANES


def _diff_attention(qt, k, vt, lam_params, gain, head_dim):
    B, W, S = qt.shape
    hw = 2 * head_dim
    H = W // hw
    tq = _pick(S, 1024, V7X_LANES)
    tk = _pick(S // 2, 512, V7X_LANES)
    blocks = (_nbytes((hw, tq), BF16) + 2 * _nbytes((S, hw), BF16) + _nbytes((tq, hw), BF16))
    scratch = (_nbytes((hw, 2 * tq), BF16) + _nbytes((hw + ONES_ROWS, S), BF16)
               + 2 * _nbytes((tk, 2 * tq), F32) + 2 * _nbytes((hw + ONES_ROWS, tq), F32))
    temps = 2 * _nbytes((tk, 2 * tq), F32)
    return pl.pallas_call(
        functools.partial(_attn_kernel, tk=tk, head_dim=head_dim),
        grid=(B, H, S // tq),
        in_specs=[pl.BlockSpec((4, head_dim), lambda b, h, i: (0, 0)),
                  pl.BlockSpec((None, hw, tq), lambda b, h, i: (b, h, i)),
                  pl.BlockSpec((None, S, hw), lambda b, h, i: (b, 0, h)),
                  pl.BlockSpec((None, hw, S), lambda b, h, i: (b, h, 0)),
                  pl.BlockSpec((hw, 1), lambda b, h, i: (0, 0))],
        out_specs=pl.BlockSpec((None, tq, hw), lambda b, h, i: (b, i, h)),
        out_shape=jax.ShapeDtypeStruct((B, S, W), BF16),
        scratch_shapes=[pltpu.VMEM((hw, 2 * tq), BF16),
                        pltpu.VMEM((hw + ONES_ROWS, S), BF16),
                        pltpu.VMEM((tk, 2 * tq), F32),
                        pltpu.VMEM((tk, 2 * tq), F32),
                        pltpu.VMEM((hw + ONES_ROWS, tq), F32),
                        pltpu.VMEM((hw + ONES_ROWS, tq), F32)],
        compiler_params=_params(("arbitrary", "arbitrary", "arbitrary"), blocks, scratch + temps),
        name="diff_attention",
    )(lam_params, qt, k, vt, gain.reshape(hw, 1).astype(F32))


def _softplus(x):
    return jnp.maximum(x, 0.0) + jnp.log1p(jnp.exp(-jnp.abs(x)))


def _rglru_kernel(*refs, reverse, n_chunks, conv_width):
    if reverse:
        (x_ref, xp_ref, xn_ref, cw_ref, cb_ref, w_ref, b_ref, lam_ref, hf_ref, y_ref,
         o_ref, xe_ref, a_ref, u_ref, h_ref) = refs
    else:
        (x_ref, xp_ref, xn_ref, cw_ref, cb_ref, w_ref, b_ref, lam_ref,
         o_ref, xe_ref, a_ref, u_ref, h_ref) = refs
    T = x_ref.shape[0]
    halo = xp_ref.shape[0]
    nblk, bw = w_ref.shape[0], w_ref.shape[1]
    step = pl.program_id(1)
    chunk = (n_chunks - 1 - step) if reverse else step

    xe_ref[0:halo, :] = jnp.where(chunk == 0, 0.0, xp_ref[...])
    xe_ref[halo:halo + T, :] = x_ref[...]
    xe_ref[halo + T:2 * halo + T, :] = jnp.where(chunk == n_chunks - 1, 0.0, xn_ref[...])
    left = conv_width // 2
    xc = cb_ref[...] + cw_ref[0:1, :] * xe_ref[halo - left:halo - left + T, :]
    for j in range(1, conv_width):
        xc = xc + cw_ref[j:j + 1, :] * xe_ref[halo - left + j:halo - left + j + T, :]

    sp = _softplus(-lam_ref[...])
    for n in range(nblk):
        cols = slice(n * bw, (n + 1) * bw)
        xb = xc[:, cols]
        z = jnp.dot(xb.astype(BF16), w_ref[n], preferred_element_type=F32) + b_ref[n]
        r = jax.nn.sigmoid(z[:, :bw])
        i = jax.nn.sigmoid(z[:, bw:])
        log_a = -RG_C * r * sp[:, cols]
        a_ref[:, cols] = jnp.exp(log_a)
        g = jnp.maximum(1.0 - jnp.exp(2.0 * log_a), 0.0)
        gain = g * lax.rsqrt(jnp.maximum(g, jnp.finfo(F32).tiny))
        u_ref[:, cols] = gain * (i * xb)

    @pl.when(step == 0)
    def _():
        h_ref[...] = jnp.zeros_like(h_ref)

    def group(g, h):
        gi = (T // V7X_SUBLANES - 1 - g) if reverse else g
        base = pl.multiple_of(gi * V7X_SUBLANES, V7X_SUBLANES)
        order = range(V7X_SUBLANES - 1, -1, -1) if reverse else range(V7X_SUBLANES)
        for r_ in order:
            row = pl.ds(base + r_, 1)
            h = a_ref[row, :] * h + u_ref[row, :]
            u_ref[row, :] = h
        return h

    h_ref[...] = lax.fori_loop(0, T // V7X_SUBLANES, group, h_ref[...])

    if reverse:
        o_ref[...] = ((hf_ref[...] + u_ref[...]) * jax.nn.gelu(y_ref[...])).astype(o_ref.dtype)
    else:
        o_ref[...] = u_ref[...]


def _rglru_pass(rest, x_col, y_col, conv_w, conv_b, wcat, bcat, lam, hf, reverse):
    B, S, _ = rest.shape
    nblk, bw, _ = wcat.shape
    C = nblk * bw
    conv_width = conv_w.shape[0]
    T = _pick(S, 256, V7X_SUBLANES)
    halo = V7X_SUBLANES
    n_chunks = S // T
    hpc = T // halo

    def cidx(c):
        return (n_chunks - 1 - c) if reverse else c

    in_specs = [
        pl.BlockSpec((None, T, C), lambda b, c: (b, cidx(c), x_col)),
        pl.BlockSpec((None, halo, C),
                     lambda b, c: (b, jnp.maximum(cidx(c) * hpc - 1, 0), x_col)),
        pl.BlockSpec((None, halo, C),
                     lambda b, c: (b, jnp.minimum((cidx(c) + 1) * hpc, S // halo - 1), x_col)),
        pl.BlockSpec((conv_width, C), lambda b, c: (0, 0)),
        pl.BlockSpec((1, C), lambda b, c: (0, 0)),
        pl.BlockSpec((nblk, bw, 2 * bw), lambda b, c: (0, 0, 0)),
        pl.BlockSpec((nblk, 1, 2 * bw), lambda b, c: (0, 0, 0)),
        pl.BlockSpec((1, C), lambda b, c: (0, 0)),
    ]
    args = [rest, rest, rest, conv_w, conv_b, wcat, bcat, lam]
    blocks = 2 * _nbytes((T, C), F32) + 2 * _nbytes((halo, C), F32) + 2 * _nbytes(wcat.shape, BF16)
    if reverse:
        in_specs += [pl.BlockSpec((None, T, C), lambda b, c: (b, cidx(c), 0)),
                     pl.BlockSpec((None, T, C), lambda b, c: (b, cidx(c), y_col))]
        args += [hf, rest]
        blocks += 2 * _nbytes((T, C), F32)
        out_dtype = BF16
    else:
        out_dtype = F32
    scratch = [pltpu.VMEM((T + 2 * halo, C), F32), pltpu.VMEM((T, C), F32),
               pltpu.VMEM((T, C), F32), pltpu.VMEM((1, C), F32)]
    return pl.pallas_call(
        functools.partial(_rglru_kernel, reverse=reverse, n_chunks=n_chunks, conv_width=conv_width),
        grid=(B, n_chunks),
        in_specs=in_specs,
        out_specs=pl.BlockSpec((None, T, C), lambda b, c: (b, cidx(c), 0)),
        out_shape=jax.ShapeDtypeStruct((B, S, C), out_dtype),
        scratch_shapes=scratch,
        compiler_params=_params(("parallel", "arbitrary"), blocks, 8 * _nbytes((T, C), F32)),
        name="rglru_bwd" if reverse else "rglru_fwd",
    )(*args)


def _merge_kernel(o_ref, r_ref, wa_ref, wr_ref, ga_ref, gr_ref, out_ref):
    a = jnp.dot(o_ref[...], wa_ref[...], preferred_element_type=F32)
    r = jnp.dot(r_ref[...], wr_ref[...], preferred_element_type=F32)
    out_ref[...] = (jax.nn.sigmoid(ga_ref[...]) * a + jax.nn.sigmoid(gr_ref[...]) * r).astype(out_ref.dtype)


def _merge(o, rr, wa, wr, rest, gate_col0):
    M, K = o.shape
    N = wa.shape[1]
    bm = _pick(M, 1024, V7X_SUBLANES)
    bn = _pick(N, 512, V7X_LANES)
    nb = N // bn
    ga0 = gate_col0 // bn
    blocks = (2 * _nbytes((bm, K), BF16) + 2 * _nbytes((K, bn), BF16) + 2 * _nbytes((bm, bn), F32)
              + _nbytes((bm, bn), BF16))
    return pl.pallas_call(
        _merge_kernel,
        grid=(M // bm, nb),
        in_specs=[pl.BlockSpec((bm, K), lambda i, j: (i, 0)),
                  pl.BlockSpec((bm, K), lambda i, j: (i, 0)),
                  pl.BlockSpec((K, bn), lambda i, j: (0, j)),
                  pl.BlockSpec((K, bn), lambda i, j: (0, j)),
                  pl.BlockSpec((bm, bn), lambda i, j: (i, ga0 + j)),
                  pl.BlockSpec((bm, bn), lambda i, j: (i, ga0 + nb + j))],
        out_specs=pl.BlockSpec((bm, bn), lambda i, j: (i, j)),
        out_shape=jax.ShapeDtypeStruct((M, N), BF16),
        compiler_params=_params(("parallel", "arbitrary"), blocks, 3 * _nbytes((bm, bn), F32)),
        name="merge",
    )(o, rr, wa, wr, rest, rest)


def _prepare_weights(ffn1_w_in, ffn1_w_out, w_in, conv_w, conv_b, rg_wa, rg_ba, rg_wx, rg_bx, rg_lambda,
                     w_attn_proj, w_rnn_proj, w_out, ffn2_w_in, ffn2_w_out, qk_width, rnn_width):
    def ffn_w(w_i, w_o):
        return w_i.astype(BF16), w_o.astype(BF16)

    c1, c2, c3 = qk_width, 2 * qk_width, 3 * qk_width
    wi = w_in.astype(BF16)
    nblk, bw = rg_wa.shape[1], rg_wa.shape[2]
    return dict(
        ffn1=ffn_w(ffn1_w_in[0], ffn1_w_out[0]),
        ffn2=ffn_w(ffn2_w_in[0], ffn2_w_out[0]),
        w_in=wi, wq_t=wi[:, :c1].T, wv_t=wi[:, c2:c3].T,
        conv_w=conv_w.astype(F32), conv_b=conv_b.reshape(1, rnn_width).astype(F32),
        wcat=jnp.concatenate([rg_wa, rg_wx], axis=-1).astype(BF16),
        bcat=jnp.concatenate([rg_ba, rg_bx], axis=-1).reshape(2, nblk, 1, 2 * bw).astype(F32),
        lam=rg_lambda.reshape(2, 1, rnn_width).astype(F32),
        w_attn=w_attn_proj.astype(BF16), w_rnn=w_rnn_proj.astype(BF16), w_out=w_out.astype(BF16),
    )


def _trunk(x, p, ffn1_norm, mix_norm, lam_params, subln_gain, ffn2_norm, final_norm, head_dim, rnn_width):
    B, S, D = x.shape
    M = B * S
    x = x.reshape(M, D)
    x = _ffn(x, ffn1_norm, *p["ffn1"])

    h = _rmsnorm(x, mix_norm, BF16)
    h3 = h.reshape(B, S, D)
    cos_t, sin_t, c, s1, s2 = _rope_tables(S, head_dim)
    qt = _proj_t(h3, p["wq_t"], rope=(cos_t, sin_t, head_dim, head_dim ** -0.5 * math.log2(math.e)))
    qk_width = p["wq_t"].shape[0]
    k = _proj_k(h3, p["w_in"], qk_width, qk_width, c, s1, s2, cos_t.shape[0])
    vt = _proj_t(h3, p["wv_t"])
    c3 = 3 * qk_width
    rest = _mm(h, p["w_in"], c3, p["w_in"].shape[1] - c3, F32)

    o = _diff_attention(qt, k, vt, lam_params, subln_gain, head_dim)

    rest3 = rest.reshape(B, S, rest.shape[1])
    rnn = (rest3, 0, 1, p["conv_w"], p["conv_b"])
    hf = _rglru_pass(*rnn, p["wcat"][0], p["bcat"][0], p["lam"][0], None, reverse=False)
    rr = _rglru_pass(*rnn, p["wcat"][1], p["bcat"][1], p["lam"][1], hf, reverse=True)

    merged = _merge(o.reshape(M, -1), rr.reshape(M, rnn_width), p["w_attn"], p["w_rnn"], rest, 2 * rnn_width)
    x = _mm_residual(merged, p["w_out"], x, 1.0)

    x = _ffn(x, ffn2_norm, *p["ffn2"])
    return _rmsnorm(x, final_norm, F32).reshape(B, S, D)


def kernel(x_prompt, x_sample, ffn1_norm, ffn1_w_in, ffn1_w_out, mix_norm, w_in, lambda_q1, lambda_k1, lambda_q2, lambda_k2, subln_gain, conv_w, conv_b, rg_wa, rg_ba, rg_wx, rg_bx, rg_lambda, w_attn_proj, w_rnn_proj, w_out, ffn2_norm, ffn2_w_in, ffn2_w_out, final_norm):
    head_dim = lambda_q1.shape[-1]
    rnn_width = conv_w.shape[-1]
    D = x_prompt.shape[-1]
    qk_width = (w_in.shape[-1] - 2 * rnn_width - 2 * D) // 3
    p = _prepare_weights(ffn1_w_in, ffn1_w_out, w_in[0], conv_w[0], conv_b[0], rg_wa[0], rg_ba[0],
                         rg_wx[0], rg_bx[0], rg_lambda[0], w_attn_proj[0], w_rnn_proj[0], w_out[0],
                         ffn2_w_in, ffn2_w_out, qk_width, rnn_width)
    lam_params = jnp.concatenate([lambda_q1, lambda_k1, lambda_q2, lambda_k2], axis=0).astype(F32)
    args = (p, ffn1_norm[0], mix_norm[0], lam_params, subln_gain[0], ffn2_norm[0], final_norm,
            head_dim, rnn_width)
    return _trunk(x_prompt, *args), _trunk(x_sample, *args)
```

```python
import functools
import math

import jax
import jax.numpy as jnp
from jax import lax
from jax.experimental import pallas as pl
from jax.experimental.pallas import tpu as pltpu

EPS = 1e-6
ROPE_THETA = 500000.0
RG_C = 8.0
LAMBDA_INIT = 0.8 - 0.6 * math.exp(-0.3 * 0)

V7X_LANES = 128
V7X_SUBLANES = 8
V7X_MXU = 256
V7X_VMEM_BYTES = 64 * 1024 * 1024
VMEM_CAP = V7X_VMEM_BYTES - 6 * 1024 * 1024

MM_BLOCK = 1024
MM_BLOCK_SMALL = 512
MM_K_BLOCK = 3072
FFN_HIDDEN_BLOCK = 512
MERGE_COL_BLOCK = 512
NORM_ROWS = 512
ATTN_Q_BLOCK = 1024
ATTN_K_CHUNK = 512
RGLRU_CHUNK = 256

F32 = jnp.float32
BF16 = jnp.bfloat16


def _pick(n, target, align):
    best = None
    for d in range(align, min(n, target) + 1, align):
        if n % d == 0:
            best = d
    return best if best is not None else n


def _nbytes(shape, dtype):
    return math.prod(shape) * jnp.dtype(dtype).itemsize


def _params(semantics, pipelined_bytes, resident_bytes=0):
    need = 2 * pipelined_bytes + resident_bytes + 4 * 1024 * 1024
    return pltpu.CompilerParams(dimension_semantics=semantics,
                                vmem_limit_bytes=int(min(max(need, 16 * 1024 * 1024), VMEM_CAP)))


def _rmsnorm_kernel(x_ref, g_ref, o_ref):
    x = x_ref[...]
    ms = jnp.mean(x * x, axis=-1, keepdims=True)
    o_ref[...] = (x * lax.rsqrt(ms + EPS) * g_ref[...]).astype(o_ref.dtype)


def _rmsnorm(x, g, out_dtype):
    M, D = x.shape
    bm = _pick(M, NORM_ROWS, V7X_SUBLANES)
    blocks = _nbytes((bm, D), F32) + _nbytes((bm, D), out_dtype)
    return pl.pallas_call(
        _rmsnorm_kernel,
        grid=(M // bm,),
        in_specs=[pl.BlockSpec((bm, D), lambda i: (i, 0)),
                  pl.BlockSpec((1, D), lambda i: (0, 0))],
        out_specs=pl.BlockSpec((bm, D), lambda i: (i, 0)),
        out_shape=jax.ShapeDtypeStruct((M, D), out_dtype),
        compiler_params=_params(("parallel",), blocks, _nbytes((bm, D), F32)),
        name="rmsnorm",
    )(x, g.reshape(1, D).astype(F32))


def _ffn_up_kernel(h_ref, wg_ref, wu_ref, o_ref):
    h = h_ref[...]
    g = jnp.dot(h, wg_ref[...], preferred_element_type=F32)
    u = jnp.dot(h, wu_ref[...], preferred_element_type=F32)
    o_ref[...] = (g * jax.nn.sigmoid(g) * u).astype(o_ref.dtype)


def _ffn_up(h, w_gu):
    M, D = h.shape
    Fp = w_gu.shape[1] // 2
    bf = _pick(Fp, FFN_HIDDEN_BLOCK, V7X_LANES)
    bm = _pick(M, MM_BLOCK * FFN_HIDDEN_BLOCK // bf, V7X_SUBLANES)
    nf = Fp // bf
    blocks = _nbytes((bm, D), BF16) + 2 * _nbytes((D, bf), BF16) + _nbytes((bm, bf), BF16)
    return pl.pallas_call(
        _ffn_up_kernel,
        grid=(M // bm, nf),
        in_specs=[pl.BlockSpec((bm, D), lambda i, j: (i, 0)),
                  pl.BlockSpec((D, bf), lambda i, j: (0, j)),
                  pl.BlockSpec((D, bf), lambda i, j: (0, nf + j))],
        out_specs=pl.BlockSpec((bm, bf), lambda i, j: (i, j)),
        out_shape=jax.ShapeDtypeStruct((M, Fp), BF16),
        compiler_params=_params(("parallel", "arbitrary"), blocks, 3 * _nbytes((bm, bf), F32)),
        name="ffn_up",
    )(h, w_gu, w_gu)


def _mm_res_kernel(a_ref, w_ref, r_ref, o_ref, *, scale):
    part = scale * jnp.dot(a_ref[...], w_ref[...], preferred_element_type=F32)

    @pl.when(pl.program_id(2) == 0)
    def _():
        o_ref[...] = r_ref[...] + part

    @pl.when(pl.program_id(2) != 0)
    def _():
        o_ref[...] += part


def _mm_residual(a, w, res, scale):
    M, K = a.shape
    N = w.shape[1]
    def footprint(bm, bn, bk):
        return _nbytes((bm, bk), BF16) + _nbytes((bk, bn), BF16) + 2 * _nbytes((bm, bn), F32)

    for t in (MM_BLOCK, MM_BLOCK_SMALL):
        bm, bn, bk = _pick(M, t, V7X_SUBLANES), _pick(N, t, V7X_LANES), K
        if 2 * footprint(bm, bn, bk) <= VMEM_CAP - 6 * 1024 * 1024:
            break
    else:
        bm, bn = _pick(M, MM_BLOCK, V7X_SUBLANES), _pick(N, MM_BLOCK, V7X_LANES)
        bk = _pick(K, MM_K_BLOCK, V7X_MXU)
    blocks = footprint(bm, bn, bk)
    return pl.pallas_call(
        functools.partial(_mm_res_kernel, scale=scale),
        grid=(M // bm, N // bn, K // bk),
        in_specs=[pl.BlockSpec((bm, bk), lambda i, j, k: (i, k)),
                  pl.BlockSpec((bk, bn), lambda i, j, k: (k, j)),
                  pl.BlockSpec((bm, bn), lambda i, j, k: (i, j))],
        out_specs=pl.BlockSpec((bm, bn), lambda i, j, k: (i, j)),
        out_shape=jax.ShapeDtypeStruct((M, N), F32),
        compiler_params=_params(("parallel", "parallel", "arbitrary"), blocks,
                                2 * _nbytes((bm, bn), F32)),
        name="mm_residual",
    )(a, w, res)


def _ffn(x, norm_g, w_gu, wo):
    h = _rmsnorm(x, norm_g, BF16)
    a = _ffn_up(h, w_gu)
    return _mm_residual(a, wo, x, 0.5)


def _mm_kernel(x_ref, w_ref, o_ref):
    o_ref[...] = jnp.dot(x_ref[...], w_ref[...], preferred_element_type=F32).astype(o_ref.dtype)


def _mm(x, w, col0, N, out_dtype):
    M, K = x.shape
    bm = _pick(M, MM_BLOCK, V7X_SUBLANES)
    bn = _pick(math.gcd(N, col0) if col0 else N, MM_BLOCK, V7X_LANES)
    j0 = col0 // bn
    blocks = _nbytes((bm, K), BF16) + _nbytes((K, bn), BF16) + _nbytes((bm, bn), out_dtype)
    return pl.pallas_call(
        _mm_kernel,
        grid=(M // bm, N // bn),
        in_specs=[pl.BlockSpec((bm, K), lambda i, j: (i, 0)),
                  pl.BlockSpec((K, bn), lambda i, j: (0, j0 + j))],
        out_specs=pl.BlockSpec((bm, bn), lambda i, j: (i, j)),
        out_shape=jax.ShapeDtypeStruct((M, N), out_dtype),
        compiler_params=_params(("parallel", "arbitrary"), blocks, _nbytes((bm, bn), F32)),
        name="mm",
    )(x, w)


def _nt_dot(w, h):
    return lax.dot_general(w, h, (((1,), (1,)), ((), ())), preferred_element_type=F32)


def _proj_vt_kernel(w_ref, h_ref, o_ref):
    o_ref[...] = _nt_dot(w_ref[...], h_ref[...]).astype(o_ref.dtype)


def _proj_qt_kernel(w_ref, h_ref, cos_ref, sin_ref, o_ref, *, head_dim, rot_half, scale):
    acc = _nt_dot(w_ref[...], h_ref[...])
    cos = cos_ref[...]
    sin = sin_ref[...]
    for g in range(acc.shape[0] // head_dim):
        base = g * head_dim
        x1 = acc[base:base + rot_half]
        x2 = acc[base + rot_half:base + 2 * rot_half]
        rest = acc[base + 2 * rot_half:base + head_dim]
        rot = jnp.concatenate([x1 * cos - x2 * sin, x2 * cos + x1 * sin, rest], axis=0)
        o_ref[base:base + head_dim, :] = (rot * scale).astype(o_ref.dtype)


def _proj_t(h3, wt, rope=None):
    B, S, K = h3.shape
    N = wt.shape[0]
    bm = _pick(S, MM_BLOCK, V7X_LANES)
    bn = _pick(N, MM_BLOCK, V7X_LANES)
    blocks = _nbytes((bm, K), BF16) + _nbytes((bn, K), BF16) + _nbytes((bn, bm), BF16)
    in_specs = [pl.BlockSpec((bn, K), lambda b, i, j: (j, 0)),
                pl.BlockSpec((None, bm, K), lambda b, i, j: (b, i, 0))]
    args = [wt, h3]
    if rope is None:
        body = _proj_vt_kernel
    else:
        cos_t, sin_t, head_dim, scale = rope
        rot_half = cos_t.shape[0]
        body = functools.partial(_proj_qt_kernel, head_dim=head_dim, rot_half=rot_half, scale=scale)
        in_specs += [pl.BlockSpec((rot_half, bm), lambda b, i, j: (0, i)),
                     pl.BlockSpec((rot_half, bm), lambda b, i, j: (0, i))]
        args += [cos_t, sin_t]
    return pl.pallas_call(
        body,
        grid=(B, S // bm, N // bn),
        in_specs=in_specs,
        out_specs=pl.BlockSpec((None, bn, bm), lambda b, i, j: (b, j, i)),
        out_shape=jax.ShapeDtypeStruct((B, N, S), BF16),
        compiler_params=_params(("parallel", "parallel", "arbitrary"), blocks,
                                2 * _nbytes((bn, bm), F32)),
        name="proj_t",
    )(*args)


def _proj_k_kernel(h_ref, w_ref, c_ref, s1_ref, s2_ref, o_ref, *, rot_half):
    acc = jnp.dot(h_ref[...], w_ref[...], preferred_element_type=F32)
    c = c_ref[...]
    s1 = s1_ref[...]
    s2 = s2_ref[...]
    for t in range(acc.shape[1] // V7X_LANES):
        x = acc[:, t * V7X_LANES:(t + 1) * V7X_LANES]
        up = pltpu.roll(x, V7X_LANES - rot_half, 1)
        dn = pltpu.roll(x, rot_half, 1)
        o_ref[:, t * V7X_LANES:(t + 1) * V7X_LANES] = (x * c + up * s1 + dn * s2).astype(o_ref.dtype)


def _proj_k(h3, w, col0, N, c, s1, s2, rot_half):
    B, S, K = h3.shape
    bm = _pick(S, MM_BLOCK, V7X_SUBLANES)
    bn = _pick(math.gcd(N, col0) if col0 else N, MM_BLOCK, V7X_LANES)
    j0 = col0 // bn
    blocks = (_nbytes((bm, K), BF16) + _nbytes((K, bn), BF16) + _nbytes((bm, bn), BF16)
              + 3 * _nbytes((bm, V7X_LANES), F32))
    tab = pl.BlockSpec((bm, V7X_LANES), lambda b, i, j: (i, 0))
    return pl.pallas_call(
        functools.partial(_proj_k_kernel, rot_half=rot_half),
        grid=(B, S // bm, N // bn),
        in_specs=[pl.BlockSpec((None, bm, K), lambda b, i, j: (b, i, 0)),
                  pl.BlockSpec((K, bn), lambda b, i, j: (0, j0 + j)),
                  tab, tab, tab],
        out_specs=pl.BlockSpec((None, bm, bn), lambda b, i, j: (b, i, j)),
        out_shape=jax.ShapeDtypeStruct((B, S, N), BF16),
        compiler_params=_params(("parallel", "parallel", "arbitrary"), blocks,
                                2 * _nbytes((bm, bn), F32)),
        name="proj_k",
    )(h3, w, c, s1, s2)


def _rope_tables(S, head_dim):
    rot_dim = head_dim // 4
    rot_half = rot_dim // 2
    inv_freq = ROPE_THETA ** (-jnp.arange(0, rot_dim, 2, dtype=F32) / rot_dim)
    ang = jnp.arange(S).astype(F32)[:, None] * inv_freq[None, :]
    cos, sin = jnp.cos(ang), jnp.sin(ang)
    d = jnp.arange(V7X_LANES) % head_dim
    idx = d % rot_half
    lo = d < rot_half
    hi = (d >= rot_half) & (d < rot_dim)
    c = jnp.where((lo | hi)[None, :], cos[:, idx], 1.0)
    s1 = jnp.where(lo[None, :], -sin[:, idx], 0.0)
    s2 = jnp.where(hi[None, :], sin[:, idx], 0.0)
    return cos.T, sin.T, c, s1, s2


def _attn_kernel(lam_ref, qt_ref, k_ref, vt_ref, gain_ref, o_ref,
                 qbd_ref, vte_ref, sa_ref, sb_ref, acc1_ref, acc2_ref, *, tk, head_dim):
    tq = qt_ref.shape[1]
    S = k_ref.shape[0]
    hw = 2 * head_dim
    n_chunks = S // tk

    @pl.when(pl.program_id(2) == 0)
    def _():
        vte_ref[0:hw, :] = vt_ref[...]
        vte_ref[hw:, :] = jnp.ones((vte_ref.shape[0] - hw, S), vte_ref.dtype)

    qbd_ref[...] = jnp.zeros_like(qbd_ref)
    qbd_ref[0:head_dim, 0:tq] = qt_ref[0:head_dim, :]
    qbd_ref[head_dim:hw, tq:2 * tq] = qt_ref[head_dim:hw, :]
    acc1_ref[...] = jnp.zeros_like(acc1_ref)
    acc2_ref[...] = jnp.zeros_like(acc2_ref)

    def scores(c, s_ref):
        start = pl.multiple_of(c * tk, tk)
        s = jnp.dot(k_ref[pl.ds(start, tk), :], qbd_ref[...], preferred_element_type=F32)
        s_ref[...] = s
        return jnp.max(s, axis=0, keepdims=True)

    def accumulate(c, s_ref, cmax, m1, m2):
        start = pl.multiple_of(c * tk, tk)
        v = vte_ref[:, pl.ds(start, tk)]
        ms = []
        for m, acc_ref, cols in ((m1, acc1_ref, slice(0, tq)), (m2, acc2_ref, slice(tq, 2 * tq))):
            m_new = jnp.maximum(m, cmax[:, cols])
            alpha = jnp.exp2(m - m_new)
            p = jnp.exp2(s_ref[:, cols] - m_new).astype(BF16)
            acc_ref[...] = alpha * acc_ref[...] + jnp.dot(v, p, preferred_element_type=F32)
            ms.append(m_new)
        return ms

    per_trip = next(u for u in (8, 4, 2) if n_chunks % u == 0)
    bufs = (sa_ref, sb_ref)

    def body(j, carry):
        m1, m2, cmax = carry
        c = per_trip * j
        for i in range(per_trip):
            cmax_next = scores(jnp.minimum(c + i + 1, n_chunks - 1), bufs[(i + 1) % 2])
            m1, m2 = accumulate(c + i, bufs[i % 2], cmax, m1, m2)
            cmax = cmax_next
        return m1, m2, cmax

    neg = jnp.full((1, tq), -jnp.inf, F32)
    lax.fori_loop(0, n_chunks // per_trip, body, (neg, neg, scores(0, sa_ref)))

    lq1, lk1, lq2, lk2 = (lam_ref[i:i + 1, :] for i in range(4))
    lam = (jnp.exp(jnp.sum(lq1 * lk1, axis=-1, keepdims=True))
           - jnp.exp(jnp.sum(lq2 * lk2, axis=-1, keepdims=True)) + LAMBDA_INIT)
    o = (acc1_ref[0:hw, :] / acc1_ref[hw:hw + 1, :]
         - lam * (acc2_ref[0:hw, :] / acc2_ref[hw:hw + 1, :]))
    ms = jnp.mean(o * o, axis=0, keepdims=True)
    o = o * lax.rsqrt(ms + EPS) * gain_ref[...] * (1.0 - LAMBDA_INIT)
    o_ref[...] = o.T.astype(o_ref.dtype)


ONES_ROWS = 2 * V7X_SUBLANES


def _diff_attention(qt, k, vt, lam_params, gain, head_dim):
    B, W, S = qt.shape
    hw = 2 * head_dim
    H = W // hw
    assert hw % V7X_LANES == 0, "one head's two maps must fill whole lane tiles"
    tq = _pick(S, ATTN_Q_BLOCK, V7X_LANES)
    tk = _pick(S // 2, ATTN_K_CHUNK, V7X_LANES)
    blocks = (_nbytes((hw, tq), BF16) + 2 * _nbytes((S, hw), BF16) + _nbytes((tq, hw), BF16))
    scratch = (_nbytes((hw, 2 * tq), BF16) + _nbytes((hw + ONES_ROWS, S), BF16)
               + 2 * _nbytes((tk, 2 * tq), F32) + 2 * _nbytes((hw + ONES_ROWS, tq), F32))
    temps = 2 * _nbytes((tk, 2 * tq), F32)
    return pl.pallas_call(
        functools.partial(_attn_kernel, tk=tk, head_dim=head_dim),
        grid=(B, H, S // tq),
        in_specs=[pl.BlockSpec((4, head_dim), lambda b, h, i: (0, 0)),
                  pl.BlockSpec((None, hw, tq), lambda b, h, i: (b, h, i)),
                  pl.BlockSpec((None, S, hw), lambda b, h, i: (b, 0, h)),
                  pl.BlockSpec((None, hw, S), lambda b, h, i: (b, h, 0)),
                  pl.BlockSpec((hw, 1), lambda b, h, i: (0, 0))],
        out_specs=pl.BlockSpec((None, tq, hw), lambda b, h, i: (b, i, h)),
        out_shape=jax.ShapeDtypeStruct((B, S, W), BF16),
        scratch_shapes=[pltpu.VMEM((hw, 2 * tq), BF16),
                        pltpu.VMEM((hw + ONES_ROWS, S), BF16),
                        pltpu.VMEM((tk, 2 * tq), F32),
                        pltpu.VMEM((tk, 2 * tq), F32),
                        pltpu.VMEM((hw + ONES_ROWS, tq), F32),
                        pltpu.VMEM((hw + ONES_ROWS, tq), F32)],
        compiler_params=_params(("arbitrary", "arbitrary", "arbitrary"), blocks, scratch + temps),
        name="diff_attention",
    )(lam_params, qt, k, vt, gain.reshape(hw, 1).astype(F32))


def _softplus(x):
    return jnp.maximum(x, 0.0) + jnp.log1p(jnp.exp(-jnp.abs(x)))


def _rglru_kernel(*refs, reverse, n_chunks, conv_width):
    if reverse:
        (x_ref, xp_ref, xn_ref, cw_ref, cb_ref, w_ref, b_ref, lam_ref, hf_ref, y_ref,
         o_ref, xe_ref, a_ref, u_ref, h_ref) = refs
    else:
        (x_ref, xp_ref, xn_ref, cw_ref, cb_ref, w_ref, b_ref, lam_ref,
         o_ref, xe_ref, a_ref, u_ref, h_ref) = refs
    T = x_ref.shape[0]
    halo = xp_ref.shape[0]
    nblk, bw = w_ref.shape[0], w_ref.shape[1]
    step = pl.program_id(1)
    chunk = (n_chunks - 1 - step) if reverse else step

    xe_ref[0:halo, :] = jnp.where(chunk == 0, 0.0, xp_ref[...])
    xe_ref[halo:halo + T, :] = x_ref[...]
    xe_ref[halo + T:2 * halo + T, :] = jnp.where(chunk == n_chunks - 1, 0.0, xn_ref[...])
    left = conv_width // 2
    xc = cb_ref[...] + cw_ref[0:1, :] * xe_ref[halo - left:halo - left + T, :]
    for j in range(1, conv_width):
        xc = xc + cw_ref[j:j + 1, :] * xe_ref[halo - left + j:halo - left + j + T, :]

    sp = _softplus(-lam_ref[...])
    for n in range(nblk):
        cols = slice(n * bw, (n + 1) * bw)
        xb = xc[:, cols]
        z = jnp.dot(xb.astype(BF16), w_ref[n], preferred_element_type=F32) + b_ref[n]
        r = jax.nn.sigmoid(z[:, :bw])
        i = jax.nn.sigmoid(z[:, bw:])
        log_a = -RG_C * r * sp[:, cols]
        a = jnp.exp(log_a)
        g = jnp.maximum(1.0 - a * a, 0.0)
        gain = g * lax.rsqrt(jnp.maximum(g, jnp.finfo(F32).tiny))
        a_ref[:, cols] = a
        u_ref[:, cols] = gain * (i * xb)

    @pl.when(step == 0)
    def _():
        h_ref[...] = jnp.zeros_like(h_ref)

    def group(g, h):
        gi = (T // V7X_SUBLANES - 1 - g) if reverse else g
        base = pl.multiple_of(gi * V7X_SUBLANES, V7X_SUBLANES)
        order = range(V7X_SUBLANES - 1, -1, -1) if reverse else range(V7X_SUBLANES)
        for r_ in order:
            row = pl.ds(base + r_, 1)
            h = a_ref[row, :] * h + u_ref[row, :]
            u_ref[row, :] = h
        return h

    h_ref[...] = lax.fori_loop(0, T // V7X_SUBLANES, group, h_ref[...])

    if reverse:
        o_ref[...] = ((hf_ref[...] + u_ref[...]) * jax.nn.gelu(y_ref[...])).astype(o_ref.dtype)
    else:
        o_ref[...] = u_ref[...]


def _rglru_pass(rest, x_col, y_col, conv_w, conv_b, wcat, bcat, lam, hf, reverse):
    B, S, _ = rest.shape
    nblk, bw, _ = wcat.shape
    C = nblk * bw
    conv_width = conv_w.shape[0]
    T = _pick(S, RGLRU_CHUNK, V7X_SUBLANES)
    halo = V7X_SUBLANES
    assert conv_width - 1 <= halo and bw % V7X_LANES == 0
    n_chunks = S // T
    hpc = T // halo

    def cidx(c):
        return (n_chunks - 1 - c) if reverse else c

    in_specs = [
        pl.BlockSpec((None, T, C), lambda b, c: (b, cidx(c), x_col)),
        pl.BlockSpec((None, halo, C),
                     lambda b, c: (b, jnp.maximum(cidx(c) * hpc - 1, 0), x_col)),
        pl.BlockSpec((None, halo, C),
                     lambda b, c: (b, jnp.minimum((cidx(c) + 1) * hpc, S // halo - 1), x_col)),
        pl.BlockSpec((conv_width, C), lambda b, c: (0, 0)),
        pl.BlockSpec((1, C), lambda b, c: (0, 0)),
        pl.BlockSpec((nblk, bw, 2 * bw), lambda b, c: (0, 0, 0)),
        pl.BlockSpec((nblk, 1, 2 * bw), lambda b, c: (0, 0, 0)),
        pl.BlockSpec((1, C), lambda b, c: (0, 0)),
    ]
    args = [rest, rest, rest, conv_w, conv_b, wcat, bcat, lam]
    blocks = 2 * _nbytes((T, C), F32) + 2 * _nbytes((halo, C), F32) + 2 * _nbytes(wcat.shape, BF16)
    if reverse:
        in_specs += [pl.BlockSpec((None, T, C), lambda b, c: (b, cidx(c), 0)),
                     pl.BlockSpec((None, T, C), lambda b, c: (b, cidx(c), y_col))]
        args += [hf, rest]
        blocks += 2 * _nbytes((T, C), F32)
        out_dtype = BF16
    else:
        out_dtype = F32
    scratch = [pltpu.VMEM((T + 2 * halo, C), F32), pltpu.VMEM((T, C), F32),
               pltpu.VMEM((T, C), F32), pltpu.VMEM((1, C), F32)]
    return pl.pallas_call(
        functools.partial(_rglru_kernel, reverse=reverse, n_chunks=n_chunks, conv_width=conv_width),
        grid=(B, n_chunks),
        in_specs=in_specs,
        out_specs=pl.BlockSpec((None, T, C), lambda b, c: (b, cidx(c), 0)),
        out_shape=jax.ShapeDtypeStruct((B, S, C), out_dtype),
        scratch_shapes=scratch,
        compiler_params=_params(("parallel", "arbitrary"), blocks, 8 * _nbytes((T, C), F32)),
        name="rglru_bwd" if reverse else "rglru_fwd",
    )(*args)


def _merge_kernel(o_ref, r_ref, wa_ref, wr_ref, ga_ref, gr_ref, out_ref):
    a = jnp.dot(o_ref[...], wa_ref[...], preferred_element_type=F32)
    r = jnp.dot(r_ref[...], wr_ref[...], preferred_element_type=F32)
    out_ref[...] = (jax.nn.sigmoid(ga_ref[...]) * a + jax.nn.sigmoid(gr_ref[...]) * r).astype(out_ref.dtype)


def _merge(o, rr, wa, wr, rest, gate_col0):
    M, K = o.shape
    N = wa.shape[1]
    bm = _pick(M, MM_BLOCK, V7X_SUBLANES)
    bn = _pick(N, MERGE_COL_BLOCK, V7X_LANES)
    nb = N // bn
    ga0 = gate_col0 // bn
    blocks = (2 * _nbytes((bm, K), BF16) + 2 * _nbytes((K, bn), BF16) + 2 * _nbytes((bm, bn), F32)
              + _nbytes((bm, bn), BF16))
    return pl.pallas_call(
        _merge_kernel,
        grid=(M // bm, nb),
        in_specs=[pl.BlockSpec((bm, K), lambda i, j: (i, 0)),
                  pl.BlockSpec((bm, K), lambda i, j: (i, 0)),
                  pl.BlockSpec((K, bn), lambda i, j: (0, j)),
                  pl.BlockSpec((K, bn), lambda i, j: (0, j)),
                  pl.BlockSpec((bm, bn), lambda i, j: (i, ga0 + j)),
                  pl.BlockSpec((bm, bn), lambda i, j: (i, ga0 + nb + j))],
        out_specs=pl.BlockSpec((bm, bn), lambda i, j: (i, j)),
        out_shape=jax.ShapeDtypeStruct((M, N), BF16),
        compiler_params=_params(("parallel", "arbitrary"), blocks, 3 * _nbytes((bm, bn), F32)),
        name="merge",
    )(o, rr, wa, wr, rest, rest)


def _prepare_weights(ffn1_w_in, ffn1_w_out, w_in, conv_w, conv_b, rg_wa, rg_ba, rg_wx, rg_bx, rg_lambda,
                     w_attn_proj, w_rnn_proj, w_out, ffn2_w_in, ffn2_w_out, qk_width, rnn_width):
    def ffn_w(w_i, w_o):
        return w_i.astype(BF16), w_o.astype(BF16)

    c1, c2, c3 = qk_width, 2 * qk_width, 3 * qk_width
    wi = w_in.astype(BF16)
    nblk, bw = rg_wa.shape[1], rg_wa.shape[2]
    return dict(
        ffn1=ffn_w(ffn1_w_in[0], ffn1_w_out[0]),
        ffn2=ffn_w(ffn2_w_in[0], ffn2_w_out[0]),
        w_in=wi, wq_t=wi[:, :c1].T, wv_t=wi[:, c2:c3].T,
        conv_w=conv_w.astype(F32), conv_b=conv_b.reshape(1, rnn_width).astype(F32),
        wcat=jnp.concatenate([rg_wa, rg_wx], axis=-1).astype(BF16),
        bcat=jnp.concatenate([rg_ba, rg_bx], axis=-1).reshape(2, nblk, 1, 2 * bw).astype(F32),
        lam=rg_lambda.reshape(2, 1, rnn_width).astype(F32),
        w_attn=w_attn_proj.astype(BF16), w_rnn=w_rnn_proj.astype(BF16), w_out=w_out.astype(BF16),
    )


def _trunk(x, p, ffn1_norm, mix_norm, lam_params, subln_gain, ffn2_norm, final_norm, head_dim, rnn_width):
    B, S, D = x.shape
    M = B * S
    x = x.reshape(M, D)
    x = _ffn(x, ffn1_norm, *p["ffn1"])

    h = _rmsnorm(x, mix_norm, BF16)
    h3 = h.reshape(B, S, D)
    cos_t, sin_t, c, s1, s2 = _rope_tables(S, head_dim)
    qt = _proj_t(h3, p["wq_t"], rope=(cos_t, sin_t, head_dim, head_dim ** -0.5 * math.log2(math.e)))
    qk_width = p["wq_t"].shape[0]
    k = _proj_k(h3, p["w_in"], qk_width, qk_width, c, s1, s2, cos_t.shape[0])
    vt = _proj_t(h3, p["wv_t"])
    c3 = 3 * qk_width
    rest = _mm(h, p["w_in"], c3, p["w_in"].shape[1] - c3, F32)

    o = _diff_attention(qt, k, vt, lam_params, subln_gain, head_dim)

    rest3 = rest.reshape(B, S, rest.shape[1])
    rnn = (rest3, 0, 1, p["conv_w"], p["conv_b"])
    hf = _rglru_pass(*rnn, p["wcat"][0], p["bcat"][0], p["lam"][0], None, reverse=False)
    rr = _rglru_pass(*rnn, p["wcat"][1], p["bcat"][1], p["lam"][1], hf, reverse=True)

    merged = _merge(o.reshape(M, -1), rr.reshape(M, rnn_width), p["w_attn"], p["w_rnn"], rest, 2 * rnn_width)
    x = _mm_residual(merged, p["w_out"], x, 1.0)

    x = _ffn(x, ffn2_norm, *p["ffn2"])
    return _rmsnorm(x, final_norm, F32).reshape(B, S, D)


def kernel(x_prompt, x_sample, ffn1_norm, ffn1_w_in, ffn1_w_out, mix_norm, w_in, lambda_q1, lambda_k1, lambda_q2, lambda_k2, subln_gain, conv_w, conv_b, rg_wa, rg_ba, rg_wx, rg_bx, rg_lambda, w_attn_proj, w_rnn_proj, w_out, ffn2_norm, ffn2_w_in, ffn2_w_out, final_norm):
    assert ffn1_norm.shape[0] == 1, "single-layer trunk"
    head_dim = lambda_q1.shape[-1]
    rnn_width = conv_w.shape[-1]
    D = x_prompt.shape[-1]
    qk_width = (w_in.shape[-1] - 2 * rnn_width - 2 * D) // 3
    p = _prepare_weights(ffn1_w_in, ffn1_w_out, w_in[0], conv_w[0], conv_b[0], rg_wa[0], rg_ba[0],
                         rg_wx[0], rg_bx[0], rg_lambda[0], w_attn_proj[0], w_rnn_proj[0], w_out[0],
                         ffn2_w_in, ffn2_w_out, qk_width, rnn_width)
    lam_params = jnp.concatenate([lambda_q1, lambda_k1, lambda_q2, lambda_k2], axis=0).astype(F32)
    args = (p, ffn1_norm[0], mix_norm[0], lam_params, subln_gain[0], ffn2_norm[0], final_norm,
            head_dim, rnn_width)
    return _trunk(x_prompt, *args), _trunk(x_sample, *args)
```

```python
import functools
import math

import jax
import jax.numpy as jnp
from jax import lax
from jax.experimental import pallas as pl
from jax.experimental.pallas import tpu as pltpu

EPS = 1e-6
ROPE_THETA = 500000.0
RG_C = 8.0
LAMBDA_INIT = 0.8 - 0.6 * math.exp(-0.3 * 0)

V7X_LANES = 128
V7X_SUBLANES = 8
V7X_MXU = 256
V7X_VMEM_BYTES = 64 * 1024 * 1024
VMEM_CAP = V7X_VMEM_BYTES - 6 * 1024 * 1024

MM_BLOCK = 1024
MM_BLOCK_SMALL = 512
MM_K_BLOCK = 3072
FFN_HIDDEN_BLOCK = 512
MERGE_COL_BLOCK = 512
NORM_ROWS = 512
ATTN_Q_BLOCK = 1024
ATTN_K_CHUNK = 512
RGLRU_CHUNK = 256

F32 = jnp.float32
BF16 = jnp.bfloat16


def _pick(n, target, align):
    best = None
    for d in range(align, min(n, target) + 1, align):
        if n % d == 0:
            best = d
    return best if best is not None else n


def _nbytes(shape, dtype):
    return math.prod(shape) * jnp.dtype(dtype).itemsize


def _params(semantics, pipelined_bytes, resident_bytes=0):
    need = 2 * pipelined_bytes + resident_bytes + 4 * 1024 * 1024
    return pltpu.CompilerParams(dimension_semantics=semantics,
                                vmem_limit_bytes=int(min(max(need, 16 * 1024 * 1024), VMEM_CAP)))


def _rmsnorm_kernel(x_ref, g_ref, o_ref):
    x = x_ref[...]
    ms = jnp.mean(x * x, axis=-1, keepdims=True)
    o_ref[...] = (x * lax.rsqrt(ms + EPS) * g_ref[...]).astype(o_ref.dtype)


def _rmsnorm(x, g, out_dtype):
    M, D = x.shape
    bm = _pick(M, NORM_ROWS, V7X_SUBLANES)
    blocks = _nbytes((bm, D), F32) + _nbytes((bm, D), out_dtype)
    return pl.pallas_call(
        _rmsnorm_kernel,
        grid=(M // bm,),
        in_specs=[pl.BlockSpec((bm, D), lambda i: (i, 0)),
                  pl.BlockSpec((1, D), lambda i: (0, 0))],
        out_specs=pl.BlockSpec((bm, D), lambda i: (i, 0)),
        out_shape=jax.ShapeDtypeStruct((M, D), out_dtype),
        compiler_params=_params(("parallel",), blocks, _nbytes((bm, D), F32)),
        name="rmsnorm",
    )(x, g.reshape(1, D).astype(F32))


def _ffn_up_kernel(h_ref, wg_ref, wu_ref, o_ref):
    h = h_ref[...]
    g = jnp.dot(h, wg_ref[...], preferred_element_type=F32)
    u = jnp.dot(h, wu_ref[...], preferred_element_type=F32)
    o_ref[...] = (g * jax.nn.sigmoid(g) * u).astype(o_ref.dtype)


def _ffn_up(h, w_gu):
    M, D = h.shape
    Fp = w_gu.shape[1] // 2
    bf = _pick(Fp, FFN_HIDDEN_BLOCK, V7X_LANES)
    bm = _pick(M, MM_BLOCK * FFN_HIDDEN_BLOCK // bf, V7X_SUBLANES)
    nf = Fp // bf
    blocks = _nbytes((bm, D), BF16) + 2 * _nbytes((D, bf), BF16) + _nbytes((bm, bf), BF16)
    return pl.pallas_call(
        _ffn_up_kernel,
        grid=(M // bm, nf),
        in_specs=[pl.BlockSpec((bm, D), lambda i, j: (i, 0)),
                  pl.BlockSpec((D, bf), lambda i, j: (0, j)),
                  pl.BlockSpec((D, bf), lambda i, j: (0, nf + j))],
        out_specs=pl.BlockSpec((bm, bf), lambda i, j: (i, j)),
        out_shape=jax.ShapeDtypeStruct((M, Fp), BF16),
        compiler_params=_params(("parallel", "arbitrary"), blocks, 3 * _nbytes((bm, bf), F32)),
        name="ffn_up",
    )(h, w_gu, w_gu)


def _mm_res_kernel(a_ref, w_ref, r_ref, o_ref, *, scale):
    part = scale * jnp.dot(a_ref[...], w_ref[...], preferred_element_type=F32)

    @pl.when(pl.program_id(2) == 0)
    def _():
        o_ref[...] = r_ref[...] + part

    @pl.when(pl.program_id(2) != 0)
    def _():
        o_ref[...] += part


def _mm_residual(a, w, res, scale):
    M, K = a.shape
    N = w.shape[1]
    def footprint(bm, bn, bk):
        return _nbytes((bm, bk), BF16) + _nbytes((bk, bn), BF16) + 2 * _nbytes((bm, bn), F32)

    for t in (MM_BLOCK, MM_BLOCK_SMALL):
        bm, bn, bk = _pick(M, t, V7X_SUBLANES), _pick(N, t, V7X_LANES), K
        if 2 * footprint(bm, bn, bk) <= VMEM_CAP - 6 * 1024 * 1024:
            break
    else:
        bm, bn = _pick(M, MM_BLOCK, V7X_SUBLANES), _pick(N, MM_BLOCK, V7X_LANES)
        bk = _pick(K, MM_K_BLOCK, V7X_MXU)
    blocks = footprint(bm, bn, bk)
    return pl.pallas_call(
        functools.partial(_mm_res_kernel, scale=scale),
        grid=(M // bm, N // bn, K // bk),
        in_specs=[pl.BlockSpec((bm, bk), lambda i, j, k: (i, k)),
                  pl.BlockSpec((bk, bn), lambda i, j, k: (k, j)),
                  pl.BlockSpec((bm, bn), lambda i, j, k: (i, j))],
        out_specs=pl.BlockSpec((bm, bn), lambda i, j, k: (i, j)),
        out_shape=jax.ShapeDtypeStruct((M, N), F32),
        compiler_params=_params(("parallel", "parallel", "arbitrary"), blocks,
                                2 * _nbytes((bm, bn), F32)),
        name="mm_residual",
    )(a, w, res)


def _ffn(x, norm_g, w_gu, wo):
    h = _rmsnorm(x, norm_g, BF16)
    a = _ffn_up(h, w_gu)
    return _mm_residual(a, wo, x, 0.5)


def _mm_kernel(x_ref, w_ref, o_ref):
    o_ref[...] = jnp.dot(x_ref[...], w_ref[...], preferred_element_type=F32).astype(o_ref.dtype)


def _mm(x, w, col0, N, out_dtype):
    M, K = x.shape
    bm = _pick(M, MM_BLOCK, V7X_SUBLANES)
    bn = _pick(math.gcd(N, col0) if col0 else N, MM_BLOCK, V7X_LANES)
    j0 = col0 // bn
    blocks = _nbytes((bm, K), BF16) + _nbytes((K, bn), BF16) + _nbytes((bm, bn), out_dtype)
    return pl.pallas_call(
        _mm_kernel,
        grid=(M // bm, N // bn),
        in_specs=[pl.BlockSpec((bm, K), lambda i, j: (i, 0)),
                  pl.BlockSpec((K, bn), lambda i, j: (0, j0 + j))],
        out_specs=pl.BlockSpec((bm, bn), lambda i, j: (i, j)),
        out_shape=jax.ShapeDtypeStruct((M, N), out_dtype),
        compiler_params=_params(("parallel", "arbitrary"), blocks, _nbytes((bm, bn), F32)),
        name="mm",
    )(x, w)


def _nt_dot(w, h):
    return lax.dot_general(w, h, (((1,), (1,)), ((), ())), preferred_element_type=F32)


def _proj_vt_kernel(w_ref, h_ref, o_ref):
    o_ref[...] = _nt_dot(w_ref[...], h_ref[...]).astype(o_ref.dtype)


def _proj_qt_kernel(w_ref, h_ref, cos_ref, sin_ref, o_ref, *, head_dim, rot_half, scale):
    acc = _nt_dot(w_ref[...], h_ref[...])
    cos = cos_ref[...]
    sin = sin_ref[...]
    for g in range(acc.shape[0] // head_dim):
        base = g * head_dim
        x1 = acc[base:base + rot_half]
        x2 = acc[base + rot_half:base + 2 * rot_half]
        rest = acc[base + 2 * rot_half:base + head_dim]
        rot = jnp.concatenate([x1 * cos - x2 * sin, x2 * cos + x1 * sin, rest], axis=0)
        o_ref[base:base + head_dim, :] = (rot * scale).astype(o_ref.dtype)


def _proj_t(h3, wt, rope=None):
    B, S, K = h3.shape
    N = wt.shape[0]
    bm = _pick(S, MM_BLOCK, V7X_LANES)
    bn = _pick(N, MM_BLOCK, V7X_LANES)
    blocks = _nbytes((bm, K), BF16) + _nbytes((bn, K), BF16) + _nbytes((bn, bm), BF16)
    in_specs = [pl.BlockSpec((bn, K), lambda b, i, j: (j, 0)),
                pl.BlockSpec((None, bm, K), lambda b, i, j: (b, i, 0))]
    args = [wt, h3]
    if rope is None:
        body = _proj_vt_kernel
    else:
        cos_t, sin_t, head_dim, scale = rope
        rot_half = cos_t.shape[0]
        body = functools.partial(_proj_qt_kernel, head_dim=head_dim, rot_half=rot_half, scale=scale)
        in_specs += [pl.BlockSpec((rot_half, bm), lambda b, i, j: (0, i)),
                     pl.BlockSpec((rot_half, bm), lambda b, i, j: (0, i))]
        args += [cos_t, sin_t]
    return pl.pallas_call(
        body,
        grid=(B, S // bm, N // bn),
        in_specs=in_specs,
        out_specs=pl.BlockSpec((None, bn, bm), lambda b, i, j: (b, j, i)),
        out_shape=jax.ShapeDtypeStruct((B, N, S), BF16),
        compiler_params=_params(("parallel", "parallel", "arbitrary"), blocks,
                                2 * _nbytes((bn, bm), F32)),
        name="proj_t",
    )(*args)


def _proj_k_kernel(h_ref, w_ref, c_ref, s1_ref, s2_ref, o_ref, *, rot_half):
    acc = jnp.dot(h_ref[...], w_ref[...], preferred_element_type=F32)
    c = c_ref[...]
    s1 = s1_ref[...]
    s2 = s2_ref[...]
    for t in range(acc.shape[1] // V7X_LANES):
        x = acc[:, t * V7X_LANES:(t + 1) * V7X_LANES]
        up = pltpu.roll(x, V7X_LANES - rot_half, 1)
        dn = pltpu.roll(x, rot_half, 1)
        o_ref[:, t * V7X_LANES:(t + 1) * V7X_LANES] = (x * c + up * s1 + dn * s2).astype(o_ref.dtype)


def _proj_k(h3, w, col0, N, c, s1, s2, rot_half):
    B, S, K = h3.shape
    bm = _pick(S, MM_BLOCK, V7X_SUBLANES)
    bn = _pick(math.gcd(N, col0) if col0 else N, MM_BLOCK, V7X_LANES)
    j0 = col0 // bn
    blocks = (_nbytes((bm, K), BF16) + _nbytes((K, bn), BF16) + _nbytes((bm, bn), BF16)
              + 3 * _nbytes((bm, V7X_LANES), F32))
    tab = pl.BlockSpec((bm, V7X_LANES), lambda b, i, j: (i, 0))
    return pl.pallas_call(
        functools.partial(_proj_k_kernel, rot_half=rot_half),
        grid=(B, S // bm, N // bn),
        in_specs=[pl.BlockSpec((None, bm, K), lambda b, i, j: (b, i, 0)),
                  pl.BlockSpec((K, bn), lambda b, i, j: (0, j0 + j)),
                  tab, tab, tab],
        out_specs=pl.BlockSpec((None, bm, bn), lambda b, i, j: (b, i, j)),
        out_shape=jax.ShapeDtypeStruct((B, S, N), BF16),
        compiler_params=_params(("parallel", "parallel", "arbitrary"), blocks,
                                2 * _nbytes((bm, bn), F32)),
        name="proj_k",
    )(h3, w, c, s1, s2)


def _rope_tables(S, head_dim):
    rot_dim = head_dim // 4
    rot_half = rot_dim // 2
    inv_freq = ROPE_THETA ** (-jnp.arange(0, rot_dim, 2, dtype=F32) / rot_dim)
    ang = jnp.arange(S).astype(F32)[:, None] * inv_freq[None, :]
    cos, sin = jnp.cos(ang), jnp.sin(ang)
    d = jnp.arange(V7X_LANES) % head_dim
    idx = d % rot_half
    lo = d < rot_half
    hi = (d >= rot_half) & (d < rot_dim)
    c = jnp.where((lo | hi)[None, :], cos[:, idx], 1.0)
    s1 = jnp.where(lo[None, :], -sin[:, idx], 0.0)
    s2 = jnp.where(hi[None, :], sin[:, idx], 0.0)
    return cos.T, sin.T, c, s1, s2


def _attn_kernel(lam_ref, qt_ref, k_ref, vt_ref, gain_ref, o_ref,
                 qbd_ref, vte_ref, sa_ref, sb_ref, acc1_ref, acc2_ref, *, tk, head_dim):
    tq = qt_ref.shape[1]
    S = k_ref.shape[0]
    hw = 2 * head_dim
    n_chunks = S // tk

    @pl.when(pl.program_id(2) == 0)
    def _():
        vte_ref[0:hw, :] = vt_ref[...]
        vte_ref[hw:, :] = jnp.ones((vte_ref.shape[0] - hw, S), vte_ref.dtype)

    qbd_ref[...] = jnp.zeros_like(qbd_ref)
    qbd_ref[0:head_dim, 0:tq] = qt_ref[0:head_dim, :]
    qbd_ref[head_dim:hw, tq:2 * tq] = qt_ref[head_dim:hw, :]
    acc1_ref[...] = jnp.zeros_like(acc1_ref)
    acc2_ref[...] = jnp.zeros_like(acc2_ref)

    def scores(c, s_ref):
        start = pl.multiple_of(c * tk, tk)
        s = jnp.dot(k_ref[pl.ds(start, tk), :], qbd_ref[...], preferred_element_type=F32)
        s_ref[...] = s
        return jnp.max(s, axis=0, keepdims=True)

    def accumulate(c, s_ref, cmax, m1, m2):
        start = pl.multiple_of(c * tk, tk)
        v = vte_ref[:, pl.ds(start, tk)]
        ms = []
        for m, acc_ref, cols in ((m1, acc1_ref, slice(0, tq)), (m2, acc2_ref, slice(tq, 2 * tq))):
            m_new = jnp.maximum(m, cmax[:, cols])
            alpha = jnp.exp2(m - m_new)
            p = jnp.exp2(s_ref[:, cols] - m_new).astype(BF16)
            acc_ref[...] = alpha * acc_ref[...] + jnp.dot(v, p, preferred_element_type=F32)
            ms.append(m_new)
        return ms

    per_trip = next(u for u in (8, 4, 2) if n_chunks % u == 0)
    bufs = (sa_ref, sb_ref)

    def body(j, carry):
        m1, m2, cmax = carry
        c = per_trip * j
        for i in range(per_trip):
            cmax_next = scores(jnp.minimum(c + i + 1, n_chunks - 1), bufs[(i + 1) % 2])
            m1, m2 = accumulate(c + i, bufs[i % 2], cmax, m1, m2)
            cmax = cmax_next
        return m1, m2, cmax

    neg = jnp.full((1, tq), -jnp.inf, F32)
    lax.fori_loop(0, n_chunks // per_trip, body, (neg, neg, scores(0, sa_ref)))

    lq1, lk1, lq2, lk2 = (lam_ref[i:i + 1, :] for i in range(4))
    lam = (jnp.exp(jnp.sum(lq1 * lk1, axis=-1, keepdims=True))
           - jnp.exp(jnp.sum(lq2 * lk2, axis=-1, keepdims=True)) + LAMBDA_INIT)
    o = (acc1_ref[0:hw, :] / acc1_ref[hw:hw + 1, :]
         - lam * (acc2_ref[0:hw, :] / acc2_ref[hw:hw + 1, :]))
    ms = jnp.mean(o * o, axis=0, keepdims=True)
    o = o * lax.rsqrt(ms + EPS) * gain_ref[...] * (1.0 - LAMBDA_INIT)
    o_ref[...] = o.T.astype(o_ref.dtype)


ONES_ROWS = 2 * V7X_SUBLANES


def _diff_attention(qt, k, vt, lam_params, gain, head_dim):
    B, W, S = qt.shape
    hw = 2 * head_dim
    H = W // hw
    assert hw % V7X_LANES == 0, "one head's two maps must fill whole lane tiles"
    tq = _pick(S, ATTN_Q_BLOCK, V7X_LANES)
    tk = _pick(S // 2, ATTN_K_CHUNK, V7X_LANES)
    blocks = (_nbytes((hw, tq), BF16) + 2 * _nbytes((S, hw), BF16) + _nbytes((tq, hw), BF16))
    scratch = (_nbytes((hw, 2 * tq), BF16) + _nbytes((hw + ONES_ROWS, S), BF16)
               + 2 * _nbytes((tk, 2 * tq), F32) + 2 * _nbytes((hw + ONES_ROWS, tq), F32))
    temps = 2 * _nbytes((tk, 2 * tq), F32)
    return pl.pallas_call(
        functools.partial(_attn_kernel, tk=tk, head_dim=head_dim),
        grid=(B, H, S // tq),
        in_specs=[pl.BlockSpec((4, head_dim), lambda b, h, i: (0, 0)),
                  pl.BlockSpec((None, hw, tq), lambda b, h, i: (b, h, i)),
                  pl.BlockSpec((None, S, hw), lambda b, h, i: (b, 0, h)),
                  pl.BlockSpec((None, hw, S), lambda b, h, i: (b, h, 0)),
                  pl.BlockSpec((hw, 1), lambda b, h, i: (0, 0))],
        out_specs=pl.BlockSpec((None, tq, hw), lambda b, h, i: (b, i, h)),
        out_shape=jax.ShapeDtypeStruct((B, S, W), BF16),
        scratch_shapes=[pltpu.VMEM((hw, 2 * tq), BF16),
                        pltpu.VMEM((hw + ONES_ROWS, S), BF16),
                        pltpu.VMEM((tk, 2 * tq), F32),
                        pltpu.VMEM((tk, 2 * tq), F32),
                        pltpu.VMEM((hw + ONES_ROWS, tq), F32),
                        pltpu.VMEM((hw + ONES_ROWS, tq), F32)],
        compiler_params=_params(("arbitrary", "arbitrary", "arbitrary"), blocks, scratch + temps),
        name="diff_attention",
    )(lam_params, qt, k, vt, gain.reshape(hw, 1).astype(F32))


def _softplus(x):
    return jnp.maximum(x, 0.0) + jnp.log1p(jnp.exp(-jnp.abs(x)))


def _rglru_kernel(*refs, reverse, n_chunks, conv_width):
    if reverse:
        (x_ref, xp_ref, xn_ref, cw_ref, cb_ref, w_ref, b_ref, lam_ref, hf_ref, y_ref,
         o_ref, xe_ref, a_ref, u_ref, h_ref) = refs
    else:
        (x_ref, xp_ref, xn_ref, cw_ref, cb_ref, w_ref, b_ref, lam_ref,
         o_ref, xe_ref, a_ref, u_ref, h_ref) = refs
    T = x_ref.shape[0]
    halo = xp_ref.shape[0]
    nblk, bw = w_ref.shape[0], w_ref.shape[1]
    step = pl.program_id(1)
    chunk = (n_chunks - 1 - step) if reverse else step

    xe_ref[0:halo, :] = jnp.where(chunk == 0, 0.0, xp_ref[...])
    xe_ref[halo:halo + T, :] = x_ref[...]
    xe_ref[halo + T:2 * halo + T, :] = jnp.where(chunk == n_chunks - 1, 0.0, xn_ref[...])
    left = conv_width // 2
    xc = cb_ref[...] + cw_ref[0:1, :] * xe_ref[halo - left:halo - left + T, :]
    for j in range(1, conv_width):
        xc = xc + cw_ref[j:j + 1, :] * xe_ref[halo - left + j:halo - left + j + T, :]

    sp = _softplus(-lam_ref[...])
    for n in range(nblk):
        cols = slice(n * bw, (n + 1) * bw)
        xb = xc[:, cols]
        z = jnp.dot(xb.astype(BF16), w_ref[n], preferred_element_type=F32) + b_ref[n]
        r = jax.nn.sigmoid(z[:, :bw])
        i = jax.nn.sigmoid(z[:, bw:])
        log_a = -RG_C * r * sp[:, cols]
        t = jnp.tanh(log_a)
        g = jnp.maximum(-2.0 * t / (1.0 - t), 0.0)
        gain = g * lax.rsqrt(jnp.maximum(g, jnp.finfo(F32).tiny))
        a_ref[:, cols] = jnp.exp(log_a)
        u_ref[:, cols] = gain * (i * xb)

    @pl.when(step == 0)
    def _():
        h_ref[...] = jnp.zeros_like(h_ref)

    def group(g, h):
        gi = (T // V7X_SUBLANES - 1 - g) if reverse else g
        base = pl.multiple_of(gi * V7X_SUBLANES, V7X_SUBLANES)
        order = range(V7X_SUBLANES - 1, -1, -1) if reverse else range(V7X_SUBLANES)
        for r_ in order:
            row = pl.ds(base + r_, 1)
            h = a_ref[row, :] * h + u_ref[row, :]
            u_ref[row, :] = h
        return h

    h_ref[...] = lax.fori_loop(0, T // V7X_SUBLANES, group, h_ref[...])

    if reverse:
        o_ref[...] = ((hf_ref[...] + u_ref[...]) * jax.nn.gelu(y_ref[...])).astype(o_ref.dtype)
    else:
        o_ref[...] = u_ref[...]


def _rglru_pass(rest, x_col, y_col, conv_w, conv_b, wcat, bcat, lam, hf, reverse):
    B, S, _ = rest.shape
    nblk, bw, _ = wcat.shape
    C = nblk * bw
    conv_width = conv_w.shape[0]
    T = _pick(S, RGLRU_CHUNK, V7X_SUBLANES)
    halo = V7X_SUBLANES
    assert conv_width - 1 <= halo and bw % V7X_LANES == 0
    n_chunks = S // T
    hpc = T // halo

    def cidx(c):
        return (n_chunks - 1 - c) if reverse else c

    in_specs = [
        pl.BlockSpec((None, T, C), lambda b, c: (b, cidx(c), x_col)),
        pl.BlockSpec((None, halo, C),
                     lambda b, c: (b, jnp.maximum(cidx(c) * hpc - 1, 0), x_col)),
        pl.BlockSpec((None, halo, C),
                     lambda b, c: (b, jnp.minimum((cidx(c) + 1) * hpc, S // halo - 1), x_col)),
        pl.BlockSpec((conv_width, C), lambda b, c: (0, 0)),
        pl.BlockSpec((1, C), lambda b, c: (0, 0)),
        pl.BlockSpec((nblk, bw, 2 * bw), lambda b, c: (0, 0, 0)),
        pl.BlockSpec((nblk, 1, 2 * bw), lambda b, c: (0, 0, 0)),
        pl.BlockSpec((1, C), lambda b, c: (0, 0)),
    ]
    args = [rest, rest, rest, conv_w, conv_b, wcat, bcat, lam]
    blocks = 2 * _nbytes((T, C), F32) + 2 * _nbytes((halo, C), F32) + 2 * _nbytes(wcat.shape, BF16)
    if reverse:
        in_specs += [pl.BlockSpec((None, T, C), lambda b, c: (b, cidx(c), 0)),
                     pl.BlockSpec((None, T, C), lambda b, c: (b, cidx(c), y_col))]
        args += [hf, rest]
        blocks += 2 * _nbytes((T, C), F32)
        out_dtype = BF16
    else:
        out_dtype = F32
    scratch = [pltpu.VMEM((T + 2 * halo, C), F32), pltpu.VMEM((T, C), F32),
               pltpu.VMEM((T, C), F32), pltpu.VMEM((1, C), F32)]
    return pl.pallas_call(
        functools.partial(_rglru_kernel, reverse=reverse, n_chunks=n_chunks, conv_width=conv_width),
        grid=(B, n_chunks),
        in_specs=in_specs,
        out_specs=pl.BlockSpec((None, T, C), lambda b, c: (b, cidx(c), 0)),
        out_shape=jax.ShapeDtypeStruct((B, S, C), out_dtype),
        scratch_shapes=scratch,
        compiler_params=_params(("parallel", "arbitrary"), blocks, 8 * _nbytes((T, C), F32)),
        name="rglru_bwd" if reverse else "rglru_fwd",
    )(*args)


def _merge_kernel(o_ref, r_ref, wa_ref, wr_ref, ga_ref, gr_ref, out_ref):
    a = jnp.dot(o_ref[...], wa_ref[...], preferred_element_type=F32)
    r = jnp.dot(r_ref[...], wr_ref[...], preferred_element_type=F32)
    out_ref[...] = (jax.nn.sigmoid(ga_ref[...]) * a + jax.nn.sigmoid(gr_ref[...]) * r).astype(out_ref.dtype)


def _merge(o, rr, wa, wr, rest, gate_col0):
    M, K = o.shape
    N = wa.shape[1]
    bm = _pick(M, MM_BLOCK, V7X_SUBLANES)
    bn = _pick(N, MERGE_COL_BLOCK, V7X_LANES)
    nb = N // bn
    ga0 = gate_col0 // bn
    blocks = (2 * _nbytes((bm, K), BF16) + 2 * _nbytes((K, bn), BF16) + 2 * _nbytes((bm, bn), F32)
              + _nbytes((bm, bn), BF16))
    return pl.pallas_call(
        _merge_kernel,
        grid=(M // bm, nb),
        in_specs=[pl.BlockSpec((bm, K), lambda i, j: (i, 0)),
                  pl.BlockSpec((bm, K), lambda i, j: (i, 0)),
                  pl.BlockSpec((K, bn), lambda i, j: (0, j)),
                  pl.BlockSpec((K, bn), lambda i, j: (0, j)),
                  pl.BlockSpec((bm, bn), lambda i, j: (i, ga0 + j)),
                  pl.BlockSpec((bm, bn), lambda i, j: (i, ga0 + nb + j))],
        out_specs=pl.BlockSpec((bm, bn), lambda i, j: (i, j)),
        out_shape=jax.ShapeDtypeStruct((M, N), BF16),
        compiler_params=_params(("parallel", "arbitrary"), blocks, 3 * _nbytes((bm, bn), F32)),
        name="merge",
    )(o, rr, wa, wr, rest, rest)


def _prepare_weights(ffn1_w_in, ffn1_w_out, w_in, conv_w, conv_b, rg_wa, rg_ba, rg_wx, rg_bx, rg_lambda,
                     w_attn_proj, w_rnn_proj, w_out, ffn2_w_in, ffn2_w_out, qk_width, rnn_width):
    def ffn_w(w_i, w_o):
        return w_i.astype(BF16), w_o.astype(BF16)

    c1, c2, c3 = qk_width, 2 * qk_width, 3 * qk_width
    wi = w_in.astype(BF16)
    nblk, bw = rg_wa.shape[1], rg_wa.shape[2]
    return dict(
        ffn1=ffn_w(ffn1_w_in[0], ffn1_w_out[0]),
        ffn2=ffn_w(ffn2_w_in[0], ffn2_w_out[0]),
        w_in=wi, wq_t=wi[:, :c1].T, wv_t=wi[:, c2:c3].T,
        conv_w=conv_w.astype(F32), conv_b=conv_b.reshape(1, rnn_width).astype(F32),
        wcat=jnp.concatenate([rg_wa, rg_wx], axis=-1).astype(BF16),
        bcat=jnp.concatenate([rg_ba, rg_bx], axis=-1).reshape(2, nblk, 1, 2 * bw).astype(F32),
        lam=rg_lambda.reshape(2, 1, rnn_width).astype(F32),
        w_attn=w_attn_proj.astype(BF16), w_rnn=w_rnn_proj.astype(BF16), w_out=w_out.astype(BF16),
    )


def _trunk(x, p, ffn1_norm, mix_norm, lam_params, subln_gain, ffn2_norm, final_norm, head_dim, rnn_width):
    B, S, D = x.shape
    M = B * S
    x = x.reshape(M, D)
    x = _ffn(x, ffn1_norm, *p["ffn1"])

    h = _rmsnorm(x, mix_norm, BF16)
    h3 = h.reshape(B, S, D)
    cos_t, sin_t, c, s1, s2 = _rope_tables(S, head_dim)
    qt = _proj_t(h3, p["wq_t"], rope=(cos_t, sin_t, head_dim, head_dim ** -0.5 * math.log2(math.e)))
    qk_width = p["wq_t"].shape[0]
    k = _proj_k(h3, p["w_in"], qk_width, qk_width, c, s1, s2, cos_t.shape[0])
    vt = _proj_t(h3, p["wv_t"])
    c3 = 3 * qk_width
    rest = _mm(h, p["w_in"], c3, p["w_in"].shape[1] - c3, F32)

    o = _diff_attention(qt, k, vt, lam_params, subln_gain, head_dim)

    rest3 = rest.reshape(B, S, rest.shape[1])
    rnn = (rest3, 0, 1, p["conv_w"], p["conv_b"])
    hf = _rglru_pass(*rnn, p["wcat"][0], p["bcat"][0], p["lam"][0], None, reverse=False)
    rr = _rglru_pass(*rnn, p["wcat"][1], p["bcat"][1], p["lam"][1], hf, reverse=True)

    merged = _merge(o.reshape(M, -1), rr.reshape(M, rnn_width), p["w_attn"], p["w_rnn"], rest, 2 * rnn_width)
    x = _mm_residual(merged, p["w_out"], x, 1.0)

    x = _ffn(x, ffn2_norm, *p["ffn2"])
    return _rmsnorm(x, final_norm, F32).reshape(B, S, D)


def kernel(x_prompt, x_sample, ffn1_norm, ffn1_w_in, ffn1_w_out, mix_norm, w_in, lambda_q1, lambda_k1, lambda_q2, lambda_k2, subln_gain, conv_w, conv_b, rg_wa, rg_ba, rg_wx, rg_bx, rg_lambda, w_attn_proj, w_rnn_proj, w_out, ffn2_norm, ffn2_w_in, ffn2_w_out, final_norm):
    assert ffn1_norm.shape[0] == 1, "single-layer trunk"
    head_dim = lambda_q1.shape[-1]
    rnn_width = conv_w.shape[-1]
    D = x_prompt.shape[-1]
    qk_width = (w_in.shape[-1] - 2 * rnn_width - 2 * D) // 3
    p = _prepare_weights(ffn1_w_in, ffn1_w_out, w_in[0], conv_w[0], conv_b[0], rg_wa[0], rg_ba[0],
                         rg_wx[0], rg_bx[0], rg_lambda[0], w_attn_proj[0], w_rnn_proj[0], w_out[0],
                         ffn2_w_in, ffn2_w_out, qk_width, rnn_width)
    lam_params = jnp.concatenate([lambda_q1, lambda_k1, lambda_q2, lambda_k2], axis=0).astype(F32)
    args = (p, ffn1_norm[0], mix_norm[0], lam_params, subln_gain[0], ffn2_norm[0], final_norm,
            head_dim, rnn_width)
    return _trunk(x_prompt, *args), _trunk(x_sample, *args)
```

```python
import functools
import math

import jax
import jax.numpy as jnp
from jax import lax
from jax.experimental import pallas as pl
from jax.experimental.pallas import tpu as pltpu

EPS = 1e-6
ROPE_THETA = 500000.0
RG_C = 8.0
LAMBDA_INIT = 0.8 - 0.6 * math.exp(-0.3 * 0)

V7X_LANES = 128
V7X_SUBLANES = 8
V7X_MXU = 256
V7X_VMEM_BYTES = 64 * 1024 * 1024
VMEM_CAP = V7X_VMEM_BYTES - 6 * 1024 * 1024

MM_BLOCK = 1024
MM_BLOCK_SMALL = 512
MM_K_BLOCK = 3072
FFN_HIDDEN_BLOCK = 512
MERGE_COL_BLOCK = 512
NORM_ROWS = 512
ATTN_Q_BLOCK = 1024
ATTN_K_CHUNK = 512
RGLRU_CHUNK = 256

F32 = jnp.float32
BF16 = jnp.bfloat16


def _pick(n, target, align):
    best = None
    for d in range(align, min(n, target) + 1, align):
        if n % d == 0:
            best = d
    return best if best is not None else n


def _nbytes(shape, dtype):
    return math.prod(shape) * jnp.dtype(dtype).itemsize


def _params(semantics, pipelined_bytes, resident_bytes=0):
    need = 2 * pipelined_bytes + resident_bytes + 4 * 1024 * 1024
    return pltpu.CompilerParams(dimension_semantics=semantics,
                                vmem_limit_bytes=int(min(max(need, 16 * 1024 * 1024), VMEM_CAP)))


def _rmsnorm_kernel(x_ref, g_ref, o_ref):
    x = x_ref[...]
    ms = jnp.mean(x * x, axis=-1, keepdims=True)
    o_ref[...] = (x * lax.rsqrt(ms + EPS) * g_ref[...]).astype(o_ref.dtype)


def _rmsnorm(x, g, out_dtype):
    M, D = x.shape
    bm = _pick(M, NORM_ROWS, V7X_SUBLANES)
    blocks = _nbytes((bm, D), F32) + _nbytes((bm, D), out_dtype)
    return pl.pallas_call(
        _rmsnorm_kernel,
        grid=(M // bm,),
        in_specs=[pl.BlockSpec((bm, D), lambda i: (i, 0)),
                  pl.BlockSpec((1, D), lambda i: (0, 0))],
        out_specs=pl.BlockSpec((bm, D), lambda i: (i, 0)),
        out_shape=jax.ShapeDtypeStruct((M, D), out_dtype),
        compiler_params=_params(("parallel",), blocks, _nbytes((bm, D), F32)),
        name="rmsnorm",
    )(x, g.reshape(1, D).astype(F32))


def _ffn_up_kernel(h_ref, wg_ref, wu_ref, o_ref):
    h = h_ref[...]
    g = jnp.dot(h, wg_ref[...], preferred_element_type=F32)
    u = jnp.dot(h, wu_ref[...], preferred_element_type=F32)
    o_ref[...] = (g * jax.nn.sigmoid(g) * u).astype(o_ref.dtype)


def _ffn_up(h, w_gu):
    M, D = h.shape
    Fp = w_gu.shape[1] // 2
    bf = _pick(Fp, FFN_HIDDEN_BLOCK, V7X_LANES)
    bm = _pick(M, MM_BLOCK * FFN_HIDDEN_BLOCK // bf, V7X_SUBLANES)
    nf = Fp // bf
    blocks = _nbytes((bm, D), BF16) + 2 * _nbytes((D, bf), BF16) + _nbytes((bm, bf), BF16)
    return pl.pallas_call(
        _ffn_up_kernel,
        grid=(M // bm, nf),
        in_specs=[pl.BlockSpec((bm, D), lambda i, j: (i, 0)),
                  pl.BlockSpec((D, bf), lambda i, j: (0, j)),
                  pl.BlockSpec((D, bf), lambda i, j: (0, nf + j))],
        out_specs=pl.BlockSpec((bm, bf), lambda i, j: (i, j)),
        out_shape=jax.ShapeDtypeStruct((M, Fp), BF16),
        compiler_params=_params(("parallel", "arbitrary"), blocks, 3 * _nbytes((bm, bf), F32)),
        name="ffn_up",
    )(h, w_gu, w_gu)


def _mm_res_kernel(a_ref, w_ref, r_ref, o_ref, *, scale):
    part = scale * jnp.dot(a_ref[...], w_ref[...], preferred_element_type=F32)

    @pl.when(pl.program_id(2) == 0)
    def _():
        o_ref[...] = r_ref[...] + part

    @pl.when(pl.program_id(2) != 0)
    def _():
        o_ref[...] += part


def _mm_residual(a, w, res, scale):
    M, K = a.shape
    N = w.shape[1]
    def footprint(bm, bn, bk):
        return _nbytes((bm, bk), BF16) + _nbytes((bk, bn), BF16) + 2 * _nbytes((bm, bn), F32)

    for t in (MM_BLOCK, MM_BLOCK_SMALL):
        bm, bn, bk = _pick(M, t, V7X_SUBLANES), _pick(N, t, V7X_LANES), K
        if 2 * footprint(bm, bn, bk) <= VMEM_CAP - 6 * 1024 * 1024:
            break
    else:
        bm, bn = _pick(M, MM_BLOCK, V7X_SUBLANES), _pick(N, MM_BLOCK, V7X_LANES)
        bk = _pick(K, MM_K_BLOCK, V7X_MXU)
    blocks = footprint(bm, bn, bk)
    return pl.pallas_call(
        functools.partial(_mm_res_kernel, scale=scale),
        grid=(M // bm, N // bn, K // bk),
        in_specs=[pl.BlockSpec((bm, bk), lambda i, j, k: (i, k)),
                  pl.BlockSpec((bk, bn), lambda i, j, k: (k, j)),
                  pl.BlockSpec((bm, bn), lambda i, j, k: (i, j))],
        out_specs=pl.BlockSpec((bm, bn), lambda i, j, k: (i, j)),
        out_shape=jax.ShapeDtypeStruct((M, N), F32),
        compiler_params=_params(("parallel", "parallel", "arbitrary"), blocks,
                                2 * _nbytes((bm, bn), F32)),
        name="mm_residual",
    )(a, w, res)


def _ffn(x, norm_g, w_gu, wo):
    h = _rmsnorm(x, norm_g, BF16)
    a = _ffn_up(h, w_gu)
    return _mm_residual(a, wo, x, 0.5)


def _mm_kernel(x_ref, w_ref, o_ref):
    o_ref[...] = jnp.dot(x_ref[...], w_ref[...], preferred_element_type=F32).astype(o_ref.dtype)


def _mm(x, w, col0, N, out_dtype):
    M, K = x.shape
    bm = _pick(M, MM_BLOCK, V7X_SUBLANES)
    bn = _pick(math.gcd(N, col0) if col0 else N, MM_BLOCK, V7X_LANES)
    j0 = col0 // bn
    blocks = _nbytes((bm, K), BF16) + _nbytes((K, bn), BF16) + _nbytes((bm, bn), out_dtype)
    return pl.pallas_call(
        _mm_kernel,
        grid=(M // bm, N // bn),
        in_specs=[pl.BlockSpec((bm, K), lambda i, j: (i, 0)),
                  pl.BlockSpec((K, bn), lambda i, j: (0, j0 + j))],
        out_specs=pl.BlockSpec((bm, bn), lambda i, j: (i, j)),
        out_shape=jax.ShapeDtypeStruct((M, N), out_dtype),
        compiler_params=_params(("parallel", "arbitrary"), blocks, _nbytes((bm, bn), F32)),
        name="mm",
    )(x, w)


def _nt_dot(w, h):
    return lax.dot_general(w, h, (((1,), (1,)), ((), ())), preferred_element_type=F32)


def _proj_vt_kernel(w_ref, h_ref, o_ref):
    o_ref[...] = _nt_dot(w_ref[...], h_ref[...]).astype(o_ref.dtype)


def _proj_qt_kernel(w_ref, h_ref, cos_ref, sin_ref, o_ref, *, head_dim, rot_half, scale):
    acc = _nt_dot(w_ref[...], h_ref[...])
    cos = cos_ref[...]
    sin = sin_ref[...]
    for g in range(acc.shape[0] // head_dim):
        base = g * head_dim
        x1 = acc[base:base + rot_half]
        x2 = acc[base + rot_half:base + 2 * rot_half]
        rest = acc[base + 2 * rot_half:base + head_dim]
        rot = jnp.concatenate([x1 * cos - x2 * sin, x2 * cos + x1 * sin, rest], axis=0)
        o_ref[base:base + head_dim, :] = (rot * scale).astype(o_ref.dtype)


def _proj_t(h3, wt, rope=None):
    B, S, K = h3.shape
    N = wt.shape[0]
    bm = _pick(S, MM_BLOCK, V7X_LANES)
    bn = _pick(N, MM_BLOCK, V7X_LANES)
    blocks = _nbytes((bm, K), BF16) + _nbytes((bn, K), BF16) + _nbytes((bn, bm), BF16)
    in_specs = [pl.BlockSpec((bn, K), lambda b, i, j: (j, 0)),
                pl.BlockSpec((None, bm, K), lambda b, i, j: (b, i, 0))]
    args = [wt, h3]
    if rope is None:
        body = _proj_vt_kernel
    else:
        cos_t, sin_t, head_dim, scale = rope
        rot_half = cos_t.shape[0]
        body = functools.partial(_proj_qt_kernel, head_dim=head_dim, rot_half=rot_half, scale=scale)
        in_specs += [pl.BlockSpec((rot_half, bm), lambda b, i, j: (0, i)),
                     pl.BlockSpec((rot_half, bm), lambda b, i, j: (0, i))]
        args += [cos_t, sin_t]
    return pl.pallas_call(
        body,
        grid=(B, S // bm, N // bn),
        in_specs=in_specs,
        out_specs=pl.BlockSpec((None, bn, bm), lambda b, i, j: (b, j, i)),
        out_shape=jax.ShapeDtypeStruct((B, N, S), BF16),
        compiler_params=_params(("parallel", "parallel", "arbitrary"), blocks,
                                2 * _nbytes((bn, bm), F32)),
        name="proj_t",
    )(*args)


def _proj_k_kernel(h_ref, w_ref, c_ref, s1_ref, s2_ref, o_ref, *, rot_half):
    acc = jnp.dot(h_ref[...], w_ref[...], preferred_element_type=F32)
    c = c_ref[...]
    s1 = s1_ref[...]
    s2 = s2_ref[...]
    for t in range(acc.shape[1] // V7X_LANES):
        x = acc[:, t * V7X_LANES:(t + 1) * V7X_LANES]
        up = pltpu.roll(x, V7X_LANES - rot_half, 1)
        dn = pltpu.roll(x, rot_half, 1)
        o_ref[:, t * V7X_LANES:(t + 1) * V7X_LANES] = (x * c + up * s1 + dn * s2).astype(o_ref.dtype)


def _proj_k(h3, w, col0, N, c, s1, s2, rot_half):
    B, S, K = h3.shape
    bm = _pick(S, MM_BLOCK, V7X_SUBLANES)
    bn = _pick(math.gcd(N, col0) if col0 else N, MM_BLOCK, V7X_LANES)
    j0 = col0 // bn
    blocks = (_nbytes((bm, K), BF16) + _nbytes((K, bn), BF16) + _nbytes((bm, bn), BF16)
              + 3 * _nbytes((bm, V7X_LANES), F32))
    tab = pl.BlockSpec((bm, V7X_LANES), lambda b, i, j: (i, 0))
    return pl.pallas_call(
        functools.partial(_proj_k_kernel, rot_half=rot_half),
        grid=(B, S // bm, N // bn),
        in_specs=[pl.BlockSpec((None, bm, K), lambda b, i, j: (b, i, 0)),
                  pl.BlockSpec((K, bn), lambda b, i, j: (0, j0 + j)),
                  tab, tab, tab],
        out_specs=pl.BlockSpec((None, bm, bn), lambda b, i, j: (b, i, j)),
        out_shape=jax.ShapeDtypeStruct((B, S, N), BF16),
        compiler_params=_params(("parallel", "parallel", "arbitrary"), blocks,
                                2 * _nbytes((bm, bn), F32)),
        name="proj_k",
    )(h3, w, c, s1, s2)


def _rope_tables(S, head_dim):
    rot_dim = head_dim // 4
    rot_half = rot_dim // 2
    inv_freq = ROPE_THETA ** (-jnp.arange(0, rot_dim, 2, dtype=F32) / rot_dim)
    ang = jnp.arange(S).astype(F32)[:, None] * inv_freq[None, :]
    cos, sin = jnp.cos(ang), jnp.sin(ang)
    d = jnp.arange(V7X_LANES) % head_dim
    idx = d % rot_half
    lo = d < rot_half
    hi = (d >= rot_half) & (d < rot_dim)
    c = jnp.where((lo | hi)[None, :], cos[:, idx], 1.0)
    s1 = jnp.where(lo[None, :], -sin[:, idx], 0.0)
    s2 = jnp.where(hi[None, :], sin[:, idx], 0.0)
    return cos.T, sin.T, c, s1, s2


def _attn_kernel(lam_ref, qt_ref, k_ref, vt_ref, gain_ref, o_ref,
                 qbd_ref, vte_ref, sa_ref, sb_ref, acc_ref, *, tk, head_dim):
    tq = qt_ref.shape[1]
    S = k_ref.shape[0]
    hw = 2 * head_dim
    n_chunks = S // tk

    @pl.when(pl.program_id(2) == 0)
    def _():
        vte_ref[0:hw, :] = vt_ref[...]
        vte_ref[hw:, :] = jnp.ones((vte_ref.shape[0] - hw, S), vte_ref.dtype)

    qbd_ref[...] = jnp.zeros_like(qbd_ref)
    qbd_ref[0:head_dim, 0:tq] = qt_ref[0:head_dim, :]
    qbd_ref[head_dim:hw, tq:2 * tq] = qt_ref[head_dim:hw, :]
    acc_ref[...] = jnp.zeros_like(acc_ref)

    def scores(c, s_ref):
        start = pl.multiple_of(c * tk, tk)
        s = jnp.dot(k_ref[pl.ds(start, tk), :], qbd_ref[...], preferred_element_type=F32)
        s_ref[...] = s
        return jnp.max(s, axis=0, keepdims=True)

    def accumulate(c, s_ref, cmax, m):
        start = pl.multiple_of(c * tk, tk)
        v = vte_ref[:, pl.ds(start, tk)]
        m_new = jnp.maximum(m, cmax)
        alpha = jnp.exp2(m - m_new)
        p = jnp.exp2(s_ref[...] - m_new).astype(BF16)
        acc_ref[...] = alpha * acc_ref[...] + jnp.dot(v, p, preferred_element_type=F32)
        return m_new

    per_trip = next(u for u in (8, 4, 2) if n_chunks % u == 0)
    bufs = (sa_ref, sb_ref)

    def body(j, carry):
        m, cmax = carry
        c = per_trip * j
        for i in range(per_trip):
            cmax_next = scores(jnp.minimum(c + i + 1, n_chunks - 1), bufs[(i + 1) % 2])
            m = accumulate(c + i, bufs[i % 2], cmax, m)
            cmax = cmax_next
        return m, cmax

    neg = jnp.full((1, 2 * tq), -jnp.inf, F32)
    lax.fori_loop(0, n_chunks // per_trip, body, (neg, scores(0, sa_ref)))

    lq1, lk1, lq2, lk2 = (lam_ref[i:i + 1, :] for i in range(4))
    lam = (jnp.exp(jnp.sum(lq1 * lk1, axis=-1, keepdims=True))
           - jnp.exp(jnp.sum(lq2 * lk2, axis=-1, keepdims=True)) + LAMBDA_INIT)
    o = (acc_ref[0:hw, 0:tq] / acc_ref[hw:hw + 1, 0:tq]
         - lam * (acc_ref[0:hw, tq:2 * tq] / acc_ref[hw:hw + 1, tq:2 * tq]))
    ms = jnp.mean(o * o, axis=0, keepdims=True)
    o = o * lax.rsqrt(ms + EPS) * gain_ref[...] * (1.0 - LAMBDA_INIT)
    o_ref[...] = o.T.astype(o_ref.dtype)


ONES_ROWS = 2 * V7X_SUBLANES


def _diff_attention(qt, k, vt, lam_params, gain, head_dim):
    B, W, S = qt.shape
    hw = 2 * head_dim
    H = W // hw
    assert hw % V7X_LANES == 0, "one head's two maps must fill whole lane tiles"
    tq = _pick(S, ATTN_Q_BLOCK, V7X_LANES)
    tk = _pick(S // 2, ATTN_K_CHUNK, V7X_LANES)
    blocks = (_nbytes((hw, tq), BF16) + 2 * _nbytes((S, hw), BF16) + _nbytes((tq, hw), BF16))
    scratch = (_nbytes((hw, 2 * tq), BF16) + _nbytes((hw + ONES_ROWS, S), BF16)
               + 2 * _nbytes((tk, 2 * tq), F32) + _nbytes((hw + ONES_ROWS, 2 * tq), F32))
    temps = 2 * _nbytes((tk, 2 * tq), F32)
    return pl.pallas_call(
        functools.partial(_attn_kernel, tk=tk, head_dim=head_dim),
        grid=(B, H, S // tq),
        in_specs=[pl.BlockSpec((4, head_dim), lambda b, h, i: (0, 0)),
                  pl.BlockSpec((None, hw, tq), lambda b, h, i: (b, h, i)),
                  pl.BlockSpec((None, S, hw), lambda b, h, i: (b, 0, h)),
                  pl.BlockSpec((None, hw, S), lambda b, h, i: (b, h, 0)),
                  pl.BlockSpec((hw, 1), lambda b, h, i: (0, 0))],
        out_specs=pl.BlockSpec((None, tq, hw), lambda b, h, i: (b, i, h)),
        out_shape=jax.ShapeDtypeStruct((B, S, W), BF16),
        scratch_shapes=[pltpu.VMEM((hw, 2 * tq), BF16),
                        pltpu.VMEM((hw + ONES_ROWS, S), BF16),
                        pltpu.VMEM((tk, 2 * tq), F32),
                        pltpu.VMEM((tk, 2 * tq), F32),
                        pltpu.VMEM((hw + ONES_ROWS, 2 * tq), F32)],
        compiler_params=_params(("arbitrary", "arbitrary", "arbitrary"), blocks, scratch + temps),
        name="diff_attention",
    )(lam_params, qt, k, vt, gain.reshape(hw, 1).astype(F32))


def _softplus(x):
    return jnp.maximum(x, 0.0) + jnp.log1p(jnp.exp(-jnp.abs(x)))


def _rglru_kernel(*refs, reverse, n_chunks, conv_width):
    if reverse:
        (x_ref, xp_ref, xn_ref, cw_ref, cb_ref, w_ref, b_ref, lam_ref, hf_ref, y_ref,
         o_ref, xe_ref, a_ref, u_ref, h_ref) = refs
    else:
        (x_ref, xp_ref, xn_ref, cw_ref, cb_ref, w_ref, b_ref, lam_ref,
         o_ref, xe_ref, a_ref, u_ref, h_ref) = refs
    T = x_ref.shape[0]
    halo = xp_ref.shape[0]
    nblk, bw = w_ref.shape[0], w_ref.shape[1]
    step = pl.program_id(1)
    chunk = (n_chunks - 1 - step) if reverse else step

    xe_ref[0:halo, :] = jnp.where(chunk == 0, 0.0, xp_ref[...])
    xe_ref[halo:halo + T, :] = x_ref[...]
    xe_ref[halo + T:2 * halo + T, :] = jnp.where(chunk == n_chunks - 1, 0.0, xn_ref[...])
    left = conv_width // 2
    xc = cb_ref[...] + cw_ref[0:1, :] * xe_ref[halo - left:halo - left + T, :]
    for j in range(1, conv_width):
        xc = xc + cw_ref[j:j + 1, :] * xe_ref[halo - left + j:halo - left + j + T, :]

    sp = _softplus(-lam_ref[...])
    for n in range(nblk):
        cols = slice(n * bw, (n + 1) * bw)
        xb = xc[:, cols]
        z = jnp.dot(xb.astype(BF16), w_ref[n], preferred_element_type=F32) + b_ref[n]
        r = jax.nn.sigmoid(z[:, :bw])
        i = jax.nn.sigmoid(z[:, bw:])
        log_a = -RG_C * r * sp[:, cols]
        t = jnp.tanh(log_a)
        g = jnp.maximum(-2.0 * t / (1.0 - t), 0.0)
        gain = g * lax.rsqrt(jnp.maximum(g, jnp.finfo(F32).tiny))
        a_ref[:, cols] = jnp.exp(log_a)
        u_ref[:, cols] = gain * (i * xb)

    @pl.when(step == 0)
    def _():
        h_ref[...] = jnp.zeros_like(h_ref)

    def group(g, h):
        gi = (T // V7X_SUBLANES - 1 - g) if reverse else g
        base = pl.multiple_of(gi * V7X_SUBLANES, V7X_SUBLANES)
        order = range(V7X_SUBLANES - 1, -1, -1) if reverse else range(V7X_SUBLANES)
        for r_ in order:
            row = pl.ds(base + r_, 1)
            h = a_ref[row, :] * h + u_ref[row, :]
            u_ref[row, :] = h
        return h

    h_ref[...] = lax.fori_loop(0, T // V7X_SUBLANES, group, h_ref[...])

    if reverse:
        o_ref[...] = ((hf_ref[...] + u_ref[...]) * jax.nn.gelu(y_ref[...])).astype(o_ref.dtype)
    else:
        o_ref[...] = u_ref[...]


def _rglru_pass(rest, x_col, y_col, conv_w, conv_b, wcat, bcat, lam, hf, reverse):
    B, S, _ = rest.shape
    nblk, bw, _ = wcat.shape
    C = nblk * bw
    conv_width = conv_w.shape[0]
    T = _pick(S, RGLRU_CHUNK, V7X_SUBLANES)
    halo = V7X_SUBLANES
    assert conv_width - 1 <= halo and bw % V7X_LANES == 0
    n_chunks = S // T
    hpc = T // halo

    def cidx(c):
        return (n_chunks - 1 - c) if reverse else c

    in_specs = [
        pl.BlockSpec((None, T, C), lambda b, c: (b, cidx(c), x_col)),
        pl.BlockSpec((None, halo, C),
                     lambda b, c: (b, jnp.maximum(cidx(c) * hpc - 1, 0), x_col)),
        pl.BlockSpec((None, halo, C),
                     lambda b, c: (b, jnp.minimum((cidx(c) + 1) * hpc, S // halo - 1), x_col)),
        pl.BlockSpec((conv_width, C), lambda b, c: (0, 0)),
        pl.BlockSpec((1, C), lambda b, c: (0, 0)),
        pl.BlockSpec((nblk, bw, 2 * bw), lambda b, c: (0, 0, 0)),
        pl.BlockSpec((nblk, 1, 2 * bw), lambda b, c: (0, 0, 0)),
        pl.BlockSpec((1, C), lambda b, c: (0, 0)),
    ]
    args = [rest, rest, rest, conv_w, conv_b, wcat, bcat, lam]
    blocks = 2 * _nbytes((T, C), F32) + 2 * _nbytes((halo, C), F32) + 2 * _nbytes(wcat.shape, BF16)
    if reverse:
        in_specs += [pl.BlockSpec((None, T, C), lambda b, c: (b, cidx(c), 0)),
                     pl.BlockSpec((None, T, C), lambda b, c: (b, cidx(c), y_col))]
        args += [hf, rest]
        blocks += 2 * _nbytes((T, C), F32)
        out_dtype = BF16
    else:
        out_dtype = F32
    scratch = [pltpu.VMEM((T + 2 * halo, C), F32), pltpu.VMEM((T, C), F32),
               pltpu.VMEM((T, C), F32), pltpu.VMEM((1, C), F32)]
    return pl.pallas_call(
        functools.partial(_rglru_kernel, reverse=reverse, n_chunks=n_chunks, conv_width=conv_width),
        grid=(B, n_chunks),
        in_specs=in_specs,
        out_specs=pl.BlockSpec((None, T, C), lambda b, c: (b, cidx(c), 0)),
        out_shape=jax.ShapeDtypeStruct((B, S, C), out_dtype),
        scratch_shapes=scratch,
        compiler_params=_params(("parallel", "arbitrary"), blocks, 8 * _nbytes((T, C), F32)),
        name="rglru_bwd" if reverse else "rglru_fwd",
    )(*args)


def _merge_kernel(o_ref, r_ref, wa_ref, wr_ref, ga_ref, gr_ref, out_ref):
    a = jnp.dot(o_ref[...], wa_ref[...], preferred_element_type=F32)
    r = jnp.dot(r_ref[...], wr_ref[...], preferred_element_type=F32)
    out_ref[...] = (jax.nn.sigmoid(ga_ref[...]) * a + jax.nn.sigmoid(gr_ref[...]) * r).astype(out_ref.dtype)


def _merge(o, rr, wa, wr, rest, gate_col0):
    M, K = o.shape
    N = wa.shape[1]
    bm = _pick(M, MM_BLOCK, V7X_SUBLANES)
    bn = _pick(N, MERGE_COL_BLOCK, V7X_LANES)
    nb = N // bn
    ga0 = gate_col0 // bn
    blocks = (2 * _nbytes((bm, K), BF16) + 2 * _nbytes((K, bn), BF16) + 2 * _nbytes((bm, bn), F32)
              + _nbytes((bm, bn), BF16))
    return pl.pallas_call(
        _merge_kernel,
        grid=(M // bm, nb),
        in_specs=[pl.BlockSpec((bm, K), lambda i, j: (i, 0)),
                  pl.BlockSpec((bm, K), lambda i, j: (i, 0)),
                  pl.BlockSpec((K, bn), lambda i, j: (0, j)),
                  pl.BlockSpec((K, bn), lambda i, j: (0, j)),
                  pl.BlockSpec((bm, bn), lambda i, j: (i, ga0 + j)),
                  pl.BlockSpec((bm, bn), lambda i, j: (i, ga0 + nb + j))],
        out_specs=pl.BlockSpec((bm, bn), lambda i, j: (i, j)),
        out_shape=jax.ShapeDtypeStruct((M, N), BF16),
        compiler_params=_params(("parallel", "arbitrary"), blocks, 3 * _nbytes((bm, bn), F32)),
        name="merge",
    )(o, rr, wa, wr, rest, rest)


def _prepare_weights(ffn1_w_in, ffn1_w_out, w_in, conv_w, conv_b, rg_wa, rg_ba, rg_wx, rg_bx, rg_lambda,
                     w_attn_proj, w_rnn_proj, w_out, ffn2_w_in, ffn2_w_out, qk_width, rnn_width):
    def ffn_w(w_i, w_o):
        return w_i.astype(BF16), w_o.astype(BF16)

    c1, c2, c3 = qk_width, 2 * qk_width, 3 * qk_width
    wi = w_in.astype(BF16)
    nblk, bw = rg_wa.shape[1], rg_wa.shape[2]
    return dict(
        ffn1=ffn_w(ffn1_w_in[0], ffn1_w_out[0]),
        ffn2=ffn_w(ffn2_w_in[0], ffn2_w_out[0]),
        w_in=wi, wq_t=wi[:, :c1].T, wv_t=wi[:, c2:c3].T,
        conv_w=conv_w.astype(F32), conv_b=conv_b.reshape(1, rnn_width).astype(F32),
        wcat=jnp.concatenate([rg_wa, rg_wx], axis=-1).astype(BF16),
        bcat=jnp.concatenate([rg_ba, rg_bx], axis=-1).reshape(2, nblk, 1, 2 * bw).astype(F32),
        lam=rg_lambda.reshape(2, 1, rnn_width).astype(F32),
        w_attn=w_attn_proj.astype(BF16), w_rnn=w_rnn_proj.astype(BF16), w_out=w_out.astype(BF16),
    )


def _trunk(x, p, ffn1_norm, mix_norm, lam_params, subln_gain, ffn2_norm, final_norm, head_dim, rnn_width):
    B, S, D = x.shape
    M = B * S
    x = x.reshape(M, D)
    x = _ffn(x, ffn1_norm, *p["ffn1"])

    h = _rmsnorm(x, mix_norm, BF16)
    h3 = h.reshape(B, S, D)
    cos_t, sin_t, c, s1, s2 = _rope_tables(S, head_dim)
    qt = _proj_t(h3, p["wq_t"], rope=(cos_t, sin_t, head_dim, head_dim ** -0.5 * math.log2(math.e)))
    qk_width = p["wq_t"].shape[0]
    k = _proj_k(h3, p["w_in"], qk_width, qk_width, c, s1, s2, cos_t.shape[0])
    vt = _proj_t(h3, p["wv_t"])
    c3 = 3 * qk_width
    rest = _mm(h, p["w_in"], c3, p["w_in"].shape[1] - c3, F32)

    o = _diff_attention(qt, k, vt, lam_params, subln_gain, head_dim)

    rest3 = rest.reshape(B, S, rest.shape[1])
    rnn = (rest3, 0, 1, p["conv_w"], p["conv_b"])
    hf = _rglru_pass(*rnn, p["wcat"][0], p["bcat"][0], p["lam"][0], None, reverse=False)
    rr = _rglru_pass(*rnn, p["wcat"][1], p["bcat"][1], p["lam"][1], hf, reverse=True)

    merged = _merge(o.reshape(M, -1), rr.reshape(M, rnn_width), p["w_attn"], p["w_rnn"], rest, 2 * rnn_width)
    x = _mm_residual(merged, p["w_out"], x, 1.0)

    x = _ffn(x, ffn2_norm, *p["ffn2"])
    return _rmsnorm(x, final_norm, F32).reshape(B, S, D)


def kernel(x_prompt, x_sample, ffn1_norm, ffn1_w_in, ffn1_w_out, mix_norm, w_in, lambda_q1, lambda_k1, lambda_q2, lambda_k2, subln_gain, conv_w, conv_b, rg_wa, rg_ba, rg_wx, rg_bx, rg_lambda, w_attn_proj, w_rnn_proj, w_out, ffn2_norm, ffn2_w_in, ffn2_w_out, final_norm):
    assert ffn1_norm.shape[0] == 1, "single-layer trunk"
    head_dim = lambda_q1.shape[-1]
    rnn_width = conv_w.shape[-1]
    D = x_prompt.shape[-1]
    qk_width = (w_in.shape[-1] - 2 * rnn_width - 2 * D) // 3
    p = _prepare_weights(ffn1_w_in, ffn1_w_out, w_in[0], conv_w[0], conv_b[0], rg_wa[0], rg_ba[0],
                         rg_wx[0], rg_bx[0], rg_lambda[0], w_attn_proj[0], w_rnn_proj[0], w_out[0],
                         ffn2_w_in, ffn2_w_out, qk_width, rnn_width)
    lam_params = jnp.concatenate([lambda_q1, lambda_k1, lambda_q2, lambda_k2], axis=0).astype(F32)
    args = (p, ffn1_norm[0], mix_norm[0], lam_params, subln_gain[0], ffn2_norm[0], final_norm,
            head_dim, rnn_width)
    return _trunk(x_prompt, *args), _trunk(x_sample, *args)
```

```python
import functools
import math

import jax
import jax.numpy as jnp
from jax import lax
from jax.experimental import pallas as pl
from jax.experimental.pallas import tpu as pltpu

EPS = 1e-6
ROPE_THETA = 500000.0
RG_C = 8.0
LAMBDA_INIT = 0.8 - 0.6 * math.exp(-0.3 * 0)

V7X_LANES = 128
V7X_SUBLANES = 8
V7X_MXU = 256
V7X_VMEM_BYTES = 64 * 1024 * 1024
VMEM_CAP = V7X_VMEM_BYTES - 6 * 1024 * 1024

MM_BLOCK = 1024
MM_BLOCK_SMALL = 512
MM_K_BLOCK = 3072
FFN_HIDDEN_BLOCK = 512
MERGE_COL_BLOCK = 512
NORM_ROWS = 512
ATTN_Q_BLOCK = 2048
ATTN_K_CHUNK = 512
RGLRU_CHUNK = 256

F32 = jnp.float32
BF16 = jnp.bfloat16


def _pick(n, target, align):
    best = None
    for d in range(align, min(n, target) + 1, align):
        if n % d == 0:
            best = d
    return best if best is not None else n


def _nbytes(shape, dtype):
    return math.prod(shape) * jnp.dtype(dtype).itemsize


def _params(semantics, pipelined_bytes, resident_bytes=0):
    need = 2 * pipelined_bytes + resident_bytes + 4 * 1024 * 1024
    return pltpu.CompilerParams(dimension_semantics=semantics,
                                vmem_limit_bytes=int(min(max(need, 16 * 1024 * 1024), VMEM_CAP)))


def _rmsnorm_kernel(x_ref, g_ref, o_ref):
    x = x_ref[...]
    ms = jnp.mean(x * x, axis=-1, keepdims=True)
    o_ref[...] = (x * lax.rsqrt(ms + EPS) * g_ref[...]).astype(o_ref.dtype)


def _rmsnorm(x, g, out_dtype):
    M, D = x.shape
    bm = _pick(M, NORM_ROWS, V7X_SUBLANES)
    blocks = _nbytes((bm, D), F32) + _nbytes((bm, D), out_dtype)
    return pl.pallas_call(
        _rmsnorm_kernel,
        grid=(M // bm,),
        in_specs=[pl.BlockSpec((bm, D), lambda i: (i, 0)),
                  pl.BlockSpec((1, D), lambda i: (0, 0))],
        out_specs=pl.BlockSpec((bm, D), lambda i: (i, 0)),
        out_shape=jax.ShapeDtypeStruct((M, D), out_dtype),
        compiler_params=_params(("parallel",), blocks, _nbytes((bm, D), F32)),
        name="rmsnorm",
    )(x, g.reshape(1, D).astype(F32))


def _ffn_up_kernel(h_ref, wg_ref, wu_ref, o_ref):
    h = h_ref[...]
    g = jnp.dot(h, wg_ref[...], preferred_element_type=F32)
    u = jnp.dot(h, wu_ref[...], preferred_element_type=F32)
    o_ref[...] = (g * jax.nn.sigmoid(g) * u).astype(o_ref.dtype)


def _ffn_up(h, w_gu):
    M, D = h.shape
    Fp = w_gu.shape[1] // 2
    bf = _pick(Fp, FFN_HIDDEN_BLOCK, V7X_LANES)
    bm = _pick(M, MM_BLOCK * FFN_HIDDEN_BLOCK // bf, V7X_SUBLANES)
    nf = Fp // bf
    blocks = _nbytes((bm, D), BF16) + 2 * _nbytes((D, bf), BF16) + _nbytes((bm, bf), BF16)
    return pl.pallas_call(
        _ffn_up_kernel,
        grid=(M // bm, nf),
        in_specs=[pl.BlockSpec((bm, D), lambda i, j: (i, 0)),
                  pl.BlockSpec((D, bf), lambda i, j: (0, j)),
                  pl.BlockSpec((D, bf), lambda i, j: (0, nf + j))],
        out_specs=pl.BlockSpec((bm, bf), lambda i, j: (i, j)),
        out_shape=jax.ShapeDtypeStruct((M, Fp), BF16),
        compiler_params=_params(("parallel", "arbitrary"), blocks, 3 * _nbytes((bm, bf), F32)),
        name="ffn_up",
    )(h, w_gu, w_gu)


def _mm_res_kernel(a_ref, w_ref, r_ref, o_ref, *, scale):
    part = scale * jnp.dot(a_ref[...], w_ref[...], preferred_element_type=F32)

    @pl.when(pl.program_id(2) == 0)
    def _():
        o_ref[...] = r_ref[...] + part

    @pl.when(pl.program_id(2) != 0)
    def _():
        o_ref[...] += part


def _mm_residual(a, w, res, scale):
    M, K = a.shape
    N = w.shape[1]
    def footprint(bm, bn, bk):
        return _nbytes((bm, bk), BF16) + _nbytes((bk, bn), BF16) + 2 * _nbytes((bm, bn), F32)

    for t in (MM_BLOCK, MM_BLOCK_SMALL):
        bm, bn, bk = _pick(M, t, V7X_SUBLANES), _pick(N, t, V7X_LANES), K
        if 2 * footprint(bm, bn, bk) <= VMEM_CAP - 6 * 1024 * 1024:
            break
    else:
        bm, bn = _pick(M, MM_BLOCK, V7X_SUBLANES), _pick(N, MM_BLOCK, V7X_LANES)
        bk = _pick(K, MM_K_BLOCK, V7X_MXU)
    blocks = footprint(bm, bn, bk)
    return pl.pallas_call(
        functools.partial(_mm_res_kernel, scale=scale),
        grid=(M // bm, N // bn, K // bk),
        in_specs=[pl.BlockSpec((bm, bk), lambda i, j, k: (i, k)),
                  pl.BlockSpec((bk, bn), lambda i, j, k: (k, j)),
                  pl.BlockSpec((bm, bn), lambda i, j, k: (i, j))],
        out_specs=pl.BlockSpec((bm, bn), lambda i, j, k: (i, j)),
        out_shape=jax.ShapeDtypeStruct((M, N), F32),
        compiler_params=_params(("parallel", "parallel", "arbitrary"), blocks,
                                2 * _nbytes((bm, bn), F32)),
        name="mm_residual",
    )(a, w, res)


def _ffn(x, norm_g, w_gu, wo):
    h = _rmsnorm(x, norm_g, BF16)
    a = _ffn_up(h, w_gu)
    return _mm_residual(a, wo, x, 0.5)


def _mm_kernel(x_ref, w_ref, o_ref):
    o_ref[...] = jnp.dot(x_ref[...], w_ref[...], preferred_element_type=F32).astype(o_ref.dtype)


def _mm(x, w, col0, N, out_dtype):
    M, K = x.shape
    bm = _pick(M, MM_BLOCK, V7X_SUBLANES)
    bn = _pick(math.gcd(N, col0) if col0 else N, MM_BLOCK, V7X_LANES)
    j0 = col0 // bn
    blocks = _nbytes((bm, K), BF16) + _nbytes((K, bn), BF16) + _nbytes((bm, bn), out_dtype)
    return pl.pallas_call(
        _mm_kernel,
        grid=(M // bm, N // bn),
        in_specs=[pl.BlockSpec((bm, K), lambda i, j: (i, 0)),
                  pl.BlockSpec((K, bn), lambda i, j: (0, j0 + j))],
        out_specs=pl.BlockSpec((bm, bn), lambda i, j: (i, j)),
        out_shape=jax.ShapeDtypeStruct((M, N), out_dtype),
        compiler_params=_params(("parallel", "arbitrary"), blocks, _nbytes((bm, bn), F32)),
        name="mm",
    )(x, w)


def _nt_dot(w, h):
    return lax.dot_general(w, h, (((1,), (1,)), ((), ())), preferred_element_type=F32)


def _proj_vt_kernel(w_ref, h_ref, o_ref):
    o_ref[...] = _nt_dot(w_ref[...], h_ref[...]).astype(o_ref.dtype)


def _proj_qt_kernel(w_ref, h_ref, cos_ref, sin_ref, o_ref, *, head_dim, rot_half, scale):
    acc = _nt_dot(w_ref[...], h_ref[...])
    cos = cos_ref[...]
    sin = sin_ref[...]
    for g in range(acc.shape[0] // head_dim):
        base = g * head_dim
        x1 = acc[base:base + rot_half]
        x2 = acc[base + rot_half:base + 2 * rot_half]
        rest = acc[base + 2 * rot_half:base + head_dim]
        rot = jnp.concatenate([x1 * cos - x2 * sin, x2 * cos + x1 * sin, rest], axis=0)
        o_ref[base:base + head_dim, :] = (rot * scale).astype(o_ref.dtype)


def _proj_t(h3, wt, rope=None):
    B, S, K = h3.shape
    N = wt.shape[0]
    bm = _pick(S, MM_BLOCK, V7X_LANES)
    bn = _pick(N, MM_BLOCK, V7X_LANES)
    blocks = _nbytes((bm, K), BF16) + _nbytes((bn, K), BF16) + _nbytes((bn, bm), BF16)
    in_specs = [pl.BlockSpec((bn, K), lambda b, i, j: (j, 0)),
                pl.BlockSpec((None, bm, K), lambda b, i, j: (b, i, 0))]
    args = [wt, h3]
    if rope is None:
        body = _proj_vt_kernel
    else:
        cos_t, sin_t, head_dim, scale = rope
        rot_half = cos_t.shape[0]
        body = functools.partial(_proj_qt_kernel, head_dim=head_dim, rot_half=rot_half, scale=scale)
        in_specs += [pl.BlockSpec((rot_half, bm), lambda b, i, j: (0, i)),
                     pl.BlockSpec((rot_half, bm), lambda b, i, j: (0, i))]
        args += [cos_t, sin_t]
    return pl.pallas_call(
        body,
        grid=(B, S // bm, N // bn),
        in_specs=in_specs,
        out_specs=pl.BlockSpec((None, bn, bm), lambda b, i, j: (b, j, i)),
        out_shape=jax.ShapeDtypeStruct((B, N, S), BF16),
        compiler_params=_params(("parallel", "parallel", "arbitrary"), blocks,
                                2 * _nbytes((bn, bm), F32)),
        name="proj_t",
    )(*args)


def _proj_k_kernel(h_ref, w_ref, c_ref, s1_ref, s2_ref, o_ref, *, rot_half):
    acc = jnp.dot(h_ref[...], w_ref[...], preferred_element_type=F32)
    c = c_ref[...]
    s1 = s1_ref[...]
    s2 = s2_ref[...]
    for t in range(acc.shape[1] // V7X_LANES):
        x = acc[:, t * V7X_LANES:(t + 1) * V7X_LANES]
        up = pltpu.roll(x, V7X_LANES - rot_half, 1)
        dn = pltpu.roll(x, rot_half, 1)
        o_ref[:, t * V7X_LANES:(t + 1) * V7X_LANES] = (x * c + up * s1 + dn * s2).astype(o_ref.dtype)


def _proj_k(h3, w, col0, N, c, s1, s2, rot_half):
    B, S, K = h3.shape
    bm = _pick(S, MM_BLOCK, V7X_SUBLANES)
    bn = _pick(math.gcd(N, col0) if col0 else N, MM_BLOCK, V7X_LANES)
    j0 = col0 // bn
    blocks = (_nbytes((bm, K), BF16) + _nbytes((K, bn), BF16) + _nbytes((bm, bn), BF16)
              + 3 * _nbytes((bm, V7X_LANES), F32))
    tab = pl.BlockSpec((bm, V7X_LANES), lambda b, i, j: (i, 0))
    return pl.pallas_call(
        functools.partial(_proj_k_kernel, rot_half=rot_half),
        grid=(B, S // bm, N // bn),
        in_specs=[pl.BlockSpec((None, bm, K), lambda b, i, j: (b, i, 0)),
                  pl.BlockSpec((K, bn), lambda b, i, j: (0, j0 + j)),
                  tab, tab, tab],
        out_specs=pl.BlockSpec((None, bm, bn), lambda b, i, j: (b, i, j)),
        out_shape=jax.ShapeDtypeStruct((B, S, N), BF16),
        compiler_params=_params(("parallel", "parallel", "arbitrary"), blocks,
                                2 * _nbytes((bm, bn), F32)),
        name="proj_k",
    )(h3, w, c, s1, s2)


def _rope_tables(S, head_dim):
    rot_dim = head_dim // 4
    rot_half = rot_dim // 2
    inv_freq = ROPE_THETA ** (-jnp.arange(0, rot_dim, 2, dtype=F32) / rot_dim)
    ang = jnp.arange(S).astype(F32)[:, None] * inv_freq[None, :]
    cos, sin = jnp.cos(ang), jnp.sin(ang)
    d = jnp.arange(V7X_LANES) % head_dim
    idx = d % rot_half
    lo = d < rot_half
    hi = (d >= rot_half) & (d < rot_dim)
    c = jnp.where((lo | hi)[None, :], cos[:, idx], 1.0)
    s1 = jnp.where(lo[None, :], -sin[:, idx], 0.0)
    s2 = jnp.where(hi[None, :], sin[:, idx], 0.0)
    return cos.T, sin.T, c, s1, s2


def _attn_kernel(lam_ref, qt_ref, k_ref, vt_ref, gain_ref, o_ref,
                 qbd_ref, vte_ref, sa_ref, sb_ref, acc_ref, *, tk, head_dim):
    tq = qt_ref.shape[1]
    S = k_ref.shape[0]
    hw = 2 * head_dim
    n_chunks = S // tk

    @pl.when(pl.program_id(2) == 0)
    def _():
        vte_ref[0:hw, :] = vt_ref[...]
        vte_ref[hw:, :] = jnp.ones((vte_ref.shape[0] - hw, S), vte_ref.dtype)

    qbd_ref[...] = jnp.zeros_like(qbd_ref)
    qbd_ref[0:head_dim, 0:tq] = qt_ref[0:head_dim, :]
    qbd_ref[head_dim:hw, tq:2 * tq] = qt_ref[head_dim:hw, :]
    acc_ref[...] = jnp.zeros_like(acc_ref)

    def scores(c, s_ref):
        start = pl.multiple_of(c * tk, tk)
        s = jnp.dot(k_ref[pl.ds(start, tk), :], qbd_ref[...], preferred_element_type=F32)
        s_ref[...] = s
        return jnp.max(s, axis=0, keepdims=True)

    def accumulate(c, s_ref, cmax, m):
        start = pl.multiple_of(c * tk, tk)
        v = vte_ref[:, pl.ds(start, tk)]
        m_new = jnp.maximum(m, cmax)
        alpha = jnp.exp2(m - m_new)
        p = jnp.exp2(s_ref[...] - m_new).astype(BF16)
        acc_ref[...] = alpha * acc_ref[...] + jnp.dot(v, p, preferred_element_type=F32)
        return m_new

    per_trip = next(u for u in (8, 4, 2) if n_chunks % u == 0)
    bufs = (sa_ref, sb_ref)

    def body(j, carry):
        m, cmax = carry
        c = per_trip * j
        for i in range(per_trip):
            cmax_next = scores(jnp.minimum(c + i + 1, n_chunks - 1), bufs[(i + 1) % 2])
            m = accumulate(c + i, bufs[i % 2], cmax, m)
            cmax = cmax_next
        return m, cmax

    neg = jnp.full((1, 2 * tq), -jnp.inf, F32)
    lax.fori_loop(0, n_chunks // per_trip, body, (neg, scores(0, sa_ref)))

    lq1, lk1, lq2, lk2 = (lam_ref[i:i + 1, :] for i in range(4))
    lam = (jnp.exp(jnp.sum(lq1 * lk1, axis=-1, keepdims=True))
           - jnp.exp(jnp.sum(lq2 * lk2, axis=-1, keepdims=True)) + LAMBDA_INIT)
    o = (acc_ref[0:hw, 0:tq] / acc_ref[hw:hw + 1, 0:tq]
         - lam * (acc_ref[0:hw, tq:2 * tq] / acc_ref[hw:hw + 1, tq:2 * tq]))
    ms = jnp.mean(o * o, axis=0, keepdims=True)
    o = o * lax.rsqrt(ms + EPS) * gain_ref[...] * (1.0 - LAMBDA_INIT)
    o_ref[...] = o.T.astype(o_ref.dtype)


ONES_ROWS = 2 * V7X_SUBLANES


def _diff_attention(qt, k, vt, lam_params, gain, head_dim):
    B, W, S = qt.shape
    hw = 2 * head_dim
    H = W // hw
    assert hw % V7X_LANES == 0, "one head's two maps must fill whole lane tiles"
    tq = _pick(S, ATTN_Q_BLOCK, V7X_LANES)
    tk = _pick(S // 2, ATTN_K_CHUNK, V7X_LANES)
    blocks = (_nbytes((hw, tq), BF16) + 2 * _nbytes((S, hw), BF16) + _nbytes((tq, hw), BF16))
    scratch = (_nbytes((hw, 2 * tq), BF16) + _nbytes((hw + ONES_ROWS, S), BF16)
               + 2 * _nbytes((tk, 2 * tq), F32) + _nbytes((hw + ONES_ROWS, 2 * tq), F32))
    temps = 2 * _nbytes((tk, 2 * tq), F32)
    return pl.pallas_call(
        functools.partial(_attn_kernel, tk=tk, head_dim=head_dim),
        grid=(B, H, S // tq),
        in_specs=[pl.BlockSpec((4, head_dim), lambda b, h, i: (0, 0)),
                  pl.BlockSpec((None, hw, tq), lambda b, h, i: (b, h, i)),
                  pl.BlockSpec((None, S, hw), lambda b, h, i: (b, 0, h)),
                  pl.BlockSpec((None, hw, S), lambda b, h, i: (b, h, 0)),
                  pl.BlockSpec((hw, 1), lambda b, h, i: (0, 0))],
        out_specs=pl.BlockSpec((None, tq, hw), lambda b, h, i: (b, i, h)),
        out_shape=jax.ShapeDtypeStruct((B, S, W), BF16),
        scratch_shapes=[pltpu.VMEM((hw, 2 * tq), BF16),
                        pltpu.VMEM((hw + ONES_ROWS, S), BF16),
                        pltpu.VMEM((tk, 2 * tq), F32),
                        pltpu.VMEM((tk, 2 * tq), F32),
                        pltpu.VMEM((hw + ONES_ROWS, 2 * tq), F32)],
        compiler_params=_params(("arbitrary", "arbitrary", "arbitrary"), blocks, scratch + temps),
        name="diff_attention",
    )(lam_params, qt, k, vt, gain.reshape(hw, 1).astype(F32))


def _softplus(x):
    return jnp.maximum(x, 0.0) + jnp.log1p(jnp.exp(-jnp.abs(x)))


def _rglru_kernel(*refs, reverse, n_chunks, conv_width):
    if reverse:
        (x_ref, xp_ref, xn_ref, cw_ref, cb_ref, w_ref, b_ref, lam_ref, hf_ref, y_ref,
         o_ref, xe_ref, a_ref, u_ref, h_ref) = refs
    else:
        (x_ref, xp_ref, xn_ref, cw_ref, cb_ref, w_ref, b_ref, lam_ref,
         o_ref, xe_ref, a_ref, u_ref, h_ref) = refs
    T = x_ref.shape[0]
    halo = xp_ref.shape[0]
    nblk, bw = w_ref.shape[0], w_ref.shape[1]
    step = pl.program_id(1)
    chunk = (n_chunks - 1 - step) if reverse else step

    xe_ref[0:halo, :] = jnp.where(chunk == 0, 0.0, xp_ref[...])
    xe_ref[halo:halo + T, :] = x_ref[...]
    xe_ref[halo + T:2 * halo + T, :] = jnp.where(chunk == n_chunks - 1, 0.0, xn_ref[...])
    left = conv_width // 2
    xc = cb_ref[...] + cw_ref[0:1, :] * xe_ref[halo - left:halo - left + T, :]
    for j in range(1, conv_width):
        xc = xc + cw_ref[j:j + 1, :] * xe_ref[halo - left + j:halo - left + j + T, :]

    sp = _softplus(-lam_ref[...])
    for n in range(nblk):
        cols = slice(n * bw, (n + 1) * bw)
        xb = xc[:, cols]
        z = jnp.dot(xb.astype(BF16), w_ref[n], preferred_element_type=F32) + b_ref[n]
        r = jax.nn.sigmoid(z[:, :bw])
        i = jax.nn.sigmoid(z[:, bw:])
        log_a = -RG_C * r * sp[:, cols]
        t = jnp.tanh(log_a)
        g = jnp.maximum(-2.0 * t / (1.0 - t), 0.0)
        gain = g * lax.rsqrt(jnp.maximum(g, jnp.finfo(F32).tiny))
        a_ref[:, cols] = jnp.exp(log_a)
        u_ref[:, cols] = gain * (i * xb)

    @pl.when(step == 0)
    def _():
        h_ref[...] = jnp.zeros_like(h_ref)

    def group(g, h):
        gi = (T // V7X_SUBLANES - 1 - g) if reverse else g
        base = pl.multiple_of(gi * V7X_SUBLANES, V7X_SUBLANES)
        order = range(V7X_SUBLANES - 1, -1, -1) if reverse else range(V7X_SUBLANES)
        for r_ in order:
            row = pl.ds(base + r_, 1)
            h = a_ref[row, :] * h + u_ref[row, :]
            u_ref[row, :] = h
        return h

    h_ref[...] = lax.fori_loop(0, T // V7X_SUBLANES, group, h_ref[...])

    if reverse:
        o_ref[...] = ((hf_ref[...] + u_ref[...]) * jax.nn.gelu(y_ref[...])).astype(o_ref.dtype)
    else:
        o_ref[...] = u_ref[...]


def _rglru_pass(rest, x_col, y_col, conv_w, conv_b, wcat, bcat, lam, hf, reverse):
    B, S, _ = rest.shape
    nblk, bw, _ = wcat.shape
    C = nblk * bw
    conv_width = conv_w.shape[0]
    T = _pick(S, RGLRU_CHUNK, V7X_SUBLANES)
    halo = V7X_SUBLANES
    assert conv_width - 1 <= halo and bw % V7X_LANES == 0
    n_chunks = S // T
    hpc = T // halo

    def cidx(c):
        return (n_chunks - 1 - c) if reverse else c

    in_specs = [
        pl.BlockSpec((None, T, C), lambda b, c: (b, cidx(c), x_col)),
        pl.BlockSpec((None, halo, C),
                     lambda b, c: (b, jnp.maximum(cidx(c) * hpc - 1, 0), x_col)),
        pl.BlockSpec((None, halo, C),
                     lambda b, c: (b, jnp.minimum((cidx(c) + 1) * hpc, S // halo - 1), x_col)),
        pl.BlockSpec((conv_width, C), lambda b, c: (0, 0)),
        pl.BlockSpec((1, C), lambda b, c: (0, 0)),
        pl.BlockSpec((nblk, bw, 2 * bw), lambda b, c: (0, 0, 0)),
        pl.BlockSpec((nblk, 1, 2 * bw), lambda b, c: (0, 0, 0)),
        pl.BlockSpec((1, C), lambda b, c: (0, 0)),
    ]
    args = [rest, rest, rest, conv_w, conv_b, wcat, bcat, lam]
    blocks = 2 * _nbytes((T, C), F32) + 2 * _nbytes((halo, C), F32) + 2 * _nbytes(wcat.shape, BF16)
    if reverse:
        in_specs += [pl.BlockSpec((None, T, C), lambda b, c: (b, cidx(c), 0)),
                     pl.BlockSpec((None, T, C), lambda b, c: (b, cidx(c), y_col))]
        args += [hf, rest]
        blocks += 2 * _nbytes((T, C), F32)
        out_dtype = BF16
    else:
        out_dtype = F32
    scratch = [pltpu.VMEM((T + 2 * halo, C), F32), pltpu.VMEM((T, C), F32),
               pltpu.VMEM((T, C), F32), pltpu.VMEM((1, C), F32)]
    return pl.pallas_call(
        functools.partial(_rglru_kernel, reverse=reverse, n_chunks=n_chunks, conv_width=conv_width),
        grid=(B, n_chunks),
        in_specs=in_specs,
        out_specs=pl.BlockSpec((None, T, C), lambda b, c: (b, cidx(c), 0)),
        out_shape=jax.ShapeDtypeStruct((B, S, C), out_dtype),
        scratch_shapes=scratch,
        compiler_params=_params(("parallel", "arbitrary"), blocks, 8 * _nbytes((T, C), F32)),
        name="rglru_bwd" if reverse else "rglru_fwd",
    )(*args)


def _merge_kernel(o_ref, r_ref, wa_ref, wr_ref, ga_ref, gr_ref, out_ref):
    a = jnp.dot(o_ref[...], wa_ref[...], preferred_element_type=F32)
    r = jnp.dot(r_ref[...], wr_ref[...], preferred_element_type=F32)
    out_ref[...] = (jax.nn.sigmoid(ga_ref[...]) * a + jax.nn.sigmoid(gr_ref[...]) * r).astype(out_ref.dtype)


def _merge(o, rr, wa, wr, rest, gate_col0):
    M, K = o.shape
    N = wa.shape[1]
    bm = _pick(M, MM_BLOCK, V7X_SUBLANES)
    bn = _pick(N, MERGE_COL_BLOCK, V7X_LANES)
    nb = N // bn
    ga0 = gate_col0 // bn
    blocks = (2 * _nbytes((bm, K), BF16) + 2 * _nbytes((K, bn), BF16) + 2 * _nbytes((bm, bn), F32)
              + _nbytes((bm, bn), BF16))
    return pl.pallas_call(
        _merge_kernel,
        grid=(M // bm, nb),
        in_specs=[pl.BlockSpec((bm, K), lambda i, j: (i, 0)),
                  pl.BlockSpec((bm, K), lambda i, j: (i, 0)),
                  pl.BlockSpec((K, bn), lambda i, j: (0, j)),
                  pl.BlockSpec((K, bn), lambda i, j: (0, j)),
                  pl.BlockSpec((bm, bn), lambda i, j: (i, ga0 + j)),
                  pl.BlockSpec((bm, bn), lambda i, j: (i, ga0 + nb + j))],
        out_specs=pl.BlockSpec((bm, bn), lambda i, j: (i, j)),
        out_shape=jax.ShapeDtypeStruct((M, N), BF16),
        compiler_params=_params(("parallel", "arbitrary"), blocks, 3 * _nbytes((bm, bn), F32)),
        name="merge",
    )(o, rr, wa, wr, rest, rest)


def _prepare_weights(ffn1_w_in, ffn1_w_out, w_in, conv_w, conv_b, rg_wa, rg_ba, rg_wx, rg_bx, rg_lambda,
                     w_attn_proj, w_rnn_proj, w_out, ffn2_w_in, ffn2_w_out, qk_width, rnn_width):
    def ffn_w(w_i, w_o):
        return w_i.astype(BF16), w_o.astype(BF16)

    c1, c2, c3 = qk_width, 2 * qk_width, 3 * qk_width
    wi = w_in.astype(BF16)
    nblk, bw = rg_wa.shape[1], rg_wa.shape[2]
    return dict(
        ffn1=ffn_w(ffn1_w_in[0], ffn1_w_out[0]),
        ffn2=ffn_w(ffn2_w_in[0], ffn2_w_out[0]),
        w_in=wi, wq_t=wi[:, :c1].T, wv_t=wi[:, c2:c3].T,
        conv_w=conv_w.astype(F32), conv_b=conv_b.reshape(1, rnn_width).astype(F32),
        wcat=jnp.concatenate([rg_wa, rg_wx], axis=-1).astype(BF16),
        bcat=jnp.concatenate([rg_ba, rg_bx], axis=-1).reshape(2, nblk, 1, 2 * bw).astype(F32),
        lam=rg_lambda.reshape(2, 1, rnn_width).astype(F32),
        w_attn=w_attn_proj.astype(BF16), w_rnn=w_rnn_proj.astype(BF16), w_out=w_out.astype(BF16),
    )


def _trunk(x, p, ffn1_norm, mix_norm, lam_params, subln_gain, ffn2_norm, final_norm, head_dim, rnn_width):
    B, S, D = x.shape
    M = B * S
    x = x.reshape(M, D)
    x = _ffn(x, ffn1_norm, *p["ffn1"])

    h = _rmsnorm(x, mix_norm, BF16)
    h3 = h.reshape(B, S, D)
    cos_t, sin_t, c, s1, s2 = _rope_tables(S, head_dim)
    qt = _proj_t(h3, p["wq_t"], rope=(cos_t, sin_t, head_dim, head_dim ** -0.5 * math.log2(math.e)))
    qk_width = p["wq_t"].shape[0]
    k = _proj_k(h3, p["w_in"], qk_width, qk_width, c, s1, s2, cos_t.shape[0])
    vt = _proj_t(h3, p["wv_t"])
    c3 = 3 * qk_width
    rest = _mm(h, p["w_in"], c3, p["w_in"].shape[1] - c3, F32)

    o = _diff_attention(qt, k, vt, lam_params, subln_gain, head_dim)

    rest3 = rest.reshape(B, S, rest.shape[1])
    rnn = (rest3, 0, 1, p["conv_w"], p["conv_b"])
    hf = _rglru_pass(*rnn, p["wcat"][0], p["bcat"][0], p["lam"][0], None, reverse=False)
    rr = _rglru_pass(*rnn, p["wcat"][1], p["bcat"][1], p["lam"][1], hf, reverse=True)

    merged = _merge(o.reshape(M, -1), rr.reshape(M, rnn_width), p["w_attn"], p["w_rnn"], rest, 2 * rnn_width)
    x = _mm_residual(merged, p["w_out"], x, 1.0)

    x = _ffn(x, ffn2_norm, *p["ffn2"])
    return _rmsnorm(x, final_norm, F32).reshape(B, S, D)


def kernel(x_prompt, x_sample, ffn1_norm, ffn1_w_in, ffn1_w_out, mix_norm, w_in, lambda_q1, lambda_k1, lambda_q2, lambda_k2, subln_gain, conv_w, conv_b, rg_wa, rg_ba, rg_wx, rg_bx, rg_lambda, w_attn_proj, w_rnn_proj, w_out, ffn2_norm, ffn2_w_in, ffn2_w_out, final_norm):
    assert ffn1_norm.shape[0] == 1, "single-layer trunk"
    head_dim = lambda_q1.shape[-1]
    rnn_width = conv_w.shape[-1]
    D = x_prompt.shape[-1]
    qk_width = (w_in.shape[-1] - 2 * rnn_width - 2 * D) // 3
    p = _prepare_weights(ffn1_w_in, ffn1_w_out, w_in[0], conv_w[0], conv_b[0], rg_wa[0], rg_ba[0],
                         rg_wx[0], rg_bx[0], rg_lambda[0], w_attn_proj[0], w_rnn_proj[0], w_out[0],
                         ffn2_w_in, ffn2_w_out, qk_width, rnn_width)
    lam_params = jnp.concatenate([lambda_q1, lambda_k1, lambda_q2, lambda_k2], axis=0).astype(F32)
    args = (p, ffn1_norm[0], mix_norm[0], lam_params, subln_gain[0], ffn2_norm[0], final_norm,
            head_dim, rnn_width)
    return _trunk(x_prompt, *args), _trunk(x_sample, *args)
```

```python
import functools
import math

import jax
import jax.numpy as jnp
from jax import lax
from jax.experimental import pallas as pl
from jax.experimental.pallas import tpu as pltpu

EPS = 1e-6
ROPE_THETA = 500000.0
RG_C = 8.0
LAMBDA_INIT = 0.8 - 0.6 * math.exp(-0.3 * 0)

V7X_LANES = 128
V7X_SUBLANES = 8
V7X_MXU = 256
V7X_VMEM_BYTES = 64 * 1024 * 1024
VMEM_CAP = V7X_VMEM_BYTES - 6 * 1024 * 1024

MM_BLOCK = 1024
MM_BLOCK_SMALL = 512
MM_K_BLOCK = 3072
FFN_HIDDEN_BLOCK = 512
MERGE_COL_BLOCK = 512
NORM_ROWS = 512
ATTN_Q_BLOCK = 1024
ATTN_K_CHUNK = 512
RGLRU_CHUNK = 256

F32 = jnp.float32
BF16 = jnp.bfloat16


def _pick(n, target, align):
    best = None
    for d in range(align, min(n, target) + 1, align):
        if n % d == 0:
            best = d
    return best if best is not None else n


def _nbytes(shape, dtype):
    return math.prod(shape) * jnp.dtype(dtype).itemsize


def _params(semantics, pipelined_bytes, resident_bytes=0):
    need = 2 * pipelined_bytes + resident_bytes + 4 * 1024 * 1024
    return pltpu.CompilerParams(dimension_semantics=semantics,
                                vmem_limit_bytes=int(min(max(need, 16 * 1024 * 1024), VMEM_CAP)))


def _rmsnorm_kernel(x_ref, g_ref, o_ref):
    x = x_ref[...]
    ms = jnp.mean(x * x, axis=-1, keepdims=True)
    o_ref[...] = (x * lax.rsqrt(ms + EPS) * g_ref[...]).astype(o_ref.dtype)


def _rmsnorm(x, g, out_dtype):
    M, D = x.shape
    bm = _pick(M, NORM_ROWS, V7X_SUBLANES)
    blocks = _nbytes((bm, D), F32) + _nbytes((bm, D), out_dtype)
    return pl.pallas_call(
        _rmsnorm_kernel,
        grid=(M // bm,),
        in_specs=[pl.BlockSpec((bm, D), lambda i: (i, 0)),
                  pl.BlockSpec((1, D), lambda i: (0, 0))],
        out_specs=pl.BlockSpec((bm, D), lambda i: (i, 0)),
        out_shape=jax.ShapeDtypeStruct((M, D), out_dtype),
        compiler_params=_params(("parallel",), blocks, _nbytes((bm, D), F32)),
        name="rmsnorm",
    )(x, g.reshape(1, D).astype(F32))


def _ffn_up_kernel(h_ref, wg_ref, wu_ref, o_ref):
    h = h_ref[...]
    g = jnp.dot(h, wg_ref[...], preferred_element_type=F32)
    u = jnp.dot(h, wu_ref[...], preferred_element_type=F32)
    o_ref[...] = (g * jax.nn.sigmoid(g) * u).astype(o_ref.dtype)


def _ffn_up(h, w_gu):
    M, D = h.shape
    Fp = w_gu.shape[1] // 2
    bf = _pick(Fp, FFN_HIDDEN_BLOCK, V7X_LANES)
    bm = _pick(M, MM_BLOCK * FFN_HIDDEN_BLOCK // bf, V7X_SUBLANES)
    nf = Fp // bf
    blocks = _nbytes((bm, D), BF16) + 2 * _nbytes((D, bf), BF16) + _nbytes((bm, bf), BF16)
    return pl.pallas_call(
        _ffn_up_kernel,
        grid=(M // bm, nf),
        in_specs=[pl.BlockSpec((bm, D), lambda i, j: (i, 0)),
                  pl.BlockSpec((D, bf), lambda i, j: (0, j)),
                  pl.BlockSpec((D, bf), lambda i, j: (0, nf + j))],
        out_specs=pl.BlockSpec((bm, bf), lambda i, j: (i, j)),
        out_shape=jax.ShapeDtypeStruct((M, Fp), BF16),
        compiler_params=_params(("parallel", "arbitrary"), blocks, 3 * _nbytes((bm, bf), F32)),
        name="ffn_up",
    )(h, w_gu, w_gu)


def _mm_res_kernel(a_ref, w_ref, r_ref, o_ref, *, scale):
    part = scale * jnp.dot(a_ref[...], w_ref[...], preferred_element_type=F32)

    @pl.when(pl.program_id(2) == 0)
    def _():
        o_ref[...] = r_ref[...] + part

    @pl.when(pl.program_id(2) != 0)
    def _():
        o_ref[...] += part


def _mm_residual(a, w, res, scale):
    M, K = a.shape
    N = w.shape[1]
    def footprint(bm, bn, bk):
        return _nbytes((bm, bk), BF16) + _nbytes((bk, bn), BF16) + 2 * _nbytes((bm, bn), F32)

    for t in (MM_BLOCK, MM_BLOCK_SMALL):
        bm, bn, bk = _pick(M, t, V7X_SUBLANES), _pick(N, t, V7X_LANES), K
        if 2 * footprint(bm, bn, bk) <= VMEM_CAP - 6 * 1024 * 1024:
            break
    else:
        bm, bn = _pick(M, MM_BLOCK, V7X_SUBLANES), _pick(N, MM_BLOCK, V7X_LANES)
        bk = _pick(K, MM_K_BLOCK, V7X_MXU)
    blocks = footprint(bm, bn, bk)
    return pl.pallas_call(
        functools.partial(_mm_res_kernel, scale=scale),
        grid=(M // bm, N // bn, K // bk),
        in_specs=[pl.BlockSpec((bm, bk), lambda i, j, k: (i, k)),
                  pl.BlockSpec((bk, bn), lambda i, j, k: (k, j)),
                  pl.BlockSpec((bm, bn), lambda i, j, k: (i, j))],
        out_specs=pl.BlockSpec((bm, bn), lambda i, j, k: (i, j)),
        out_shape=jax.ShapeDtypeStruct((M, N), F32),
        compiler_params=_params(("parallel", "parallel", "arbitrary"), blocks,
                                2 * _nbytes((bm, bn), F32)),
        name="mm_residual",
    )(a, w, res)


def _ffn(x, norm_g, w_gu, wo):
    h = _rmsnorm(x, norm_g, BF16)
    a = _ffn_up(h, w_gu)
    return _mm_residual(a, wo, x, 0.5)


def _mm_kernel(x_ref, w_ref, o_ref):
    o_ref[...] = jnp.dot(x_ref[...], w_ref[...], preferred_element_type=F32).astype(o_ref.dtype)


def _mm(x, w, col0, N, out_dtype):
    M, K = x.shape
    bm = _pick(M, MM_BLOCK, V7X_SUBLANES)
    bn = _pick(math.gcd(N, col0) if col0 else N, MM_BLOCK, V7X_LANES)
    j0 = col0 // bn
    blocks = _nbytes((bm, K), BF16) + _nbytes((K, bn), BF16) + _nbytes((bm, bn), out_dtype)
    return pl.pallas_call(
        _mm_kernel,
        grid=(M // bm, N // bn),
        in_specs=[pl.BlockSpec((bm, K), lambda i, j: (i, 0)),
                  pl.BlockSpec((K, bn), lambda i, j: (0, j0 + j))],
        out_specs=pl.BlockSpec((bm, bn), lambda i, j: (i, j)),
        out_shape=jax.ShapeDtypeStruct((M, N), out_dtype),
        compiler_params=_params(("parallel", "arbitrary"), blocks, _nbytes((bm, bn), F32)),
        name="mm",
    )(x, w)


def _nt_dot(w, h):
    return lax.dot_general(w, h, (((1,), (1,)), ((), ())), preferred_element_type=F32)


def _proj_vt_kernel(w_ref, h_ref, o_ref):
    o_ref[...] = _nt_dot(w_ref[...], h_ref[...]).astype(o_ref.dtype)


def _proj_qt_kernel(w_ref, h_ref, cos_ref, sin_ref, o_ref, *, head_dim, rot_half, scale):
    acc = _nt_dot(w_ref[...], h_ref[...])
    cos = cos_ref[...]
    sin = sin_ref[...]
    for g in range(acc.shape[0] // head_dim):
        base = g * head_dim
        x1 = acc[base:base + rot_half]
        x2 = acc[base + rot_half:base + 2 * rot_half]
        rest = acc[base + 2 * rot_half:base + head_dim]
        rot = jnp.concatenate([x1 * cos - x2 * sin, x2 * cos + x1 * sin, rest], axis=0)
        o_ref[base:base + head_dim, :] = (rot * scale).astype(o_ref.dtype)


def _proj_t(h3, wt, rope=None):
    B, S, K = h3.shape
    N = wt.shape[0]
    bm = _pick(S, MM_BLOCK, V7X_LANES)
    bn = _pick(N, MM_BLOCK, V7X_LANES)
    blocks = _nbytes((bm, K), BF16) + _nbytes((bn, K), BF16) + _nbytes((bn, bm), BF16)
    in_specs = [pl.BlockSpec((bn, K), lambda b, i, j: (j, 0)),
                pl.BlockSpec((None, bm, K), lambda b, i, j: (b, i, 0))]
    args = [wt, h3]
    if rope is None:
        body = _proj_vt_kernel
    else:
        cos_t, sin_t, head_dim, scale = rope
        rot_half = cos_t.shape[0]
        body = functools.partial(_proj_qt_kernel, head_dim=head_dim, rot_half=rot_half, scale=scale)
        in_specs += [pl.BlockSpec((rot_half, bm), lambda b, i, j: (0, i)),
                     pl.BlockSpec((rot_half, bm), lambda b, i, j: (0, i))]
        args += [cos_t, sin_t]
    return pl.pallas_call(
        body,
        grid=(B, S // bm, N // bn),
        in_specs=in_specs,
        out_specs=pl.BlockSpec((None, bn, bm), lambda b, i, j: (b, j, i)),
        out_shape=jax.ShapeDtypeStruct((B, N, S), BF16),
        compiler_params=_params(("parallel", "parallel", "arbitrary"), blocks,
                                2 * _nbytes((bn, bm), F32)),
        name="proj_t",
    )(*args)


def _proj_k_kernel(h_ref, w_ref, c_ref, s1_ref, s2_ref, o_ref, *, rot_half):
    acc = jnp.dot(h_ref[...], w_ref[...], preferred_element_type=F32)
    c = c_ref[...]
    s1 = s1_ref[...]
    s2 = s2_ref[...]
    for t in range(acc.shape[1] // V7X_LANES):
        x = acc[:, t * V7X_LANES:(t + 1) * V7X_LANES]
        up = pltpu.roll(x, V7X_LANES - rot_half, 1)
        dn = pltpu.roll(x, rot_half, 1)
        o_ref[:, t * V7X_LANES:(t + 1) * V7X_LANES] = (x * c + up * s1 + dn * s2).astype(o_ref.dtype)


def _proj_k(h3, w, col0, N, c, s1, s2, rot_half):
    B, S, K = h3.shape
    bm = _pick(S, MM_BLOCK, V7X_SUBLANES)
    bn = _pick(math.gcd(N, col0) if col0 else N, MM_BLOCK, V7X_LANES)
    j0 = col0 // bn
    blocks = (_nbytes((bm, K), BF16) + _nbytes((K, bn), BF16) + _nbytes((bm, bn), BF16)
              + 3 * _nbytes((bm, V7X_LANES), F32))
    tab = pl.BlockSpec((bm, V7X_LANES), lambda b, i, j: (i, 0))
    return pl.pallas_call(
        functools.partial(_proj_k_kernel, rot_half=rot_half),
        grid=(B, S // bm, N // bn),
        in_specs=[pl.BlockSpec((None, bm, K), lambda b, i, j: (b, i, 0)),
                  pl.BlockSpec((K, bn), lambda b, i, j: (0, j0 + j)),
                  tab, tab, tab],
        out_specs=pl.BlockSpec((None, bm, bn), lambda b, i, j: (b, i, j)),
        out_shape=jax.ShapeDtypeStruct((B, S, N), BF16),
        compiler_params=_params(("parallel", "parallel", "arbitrary"), blocks,
                                2 * _nbytes((bm, bn), F32)),
        name="proj_k",
    )(h3, w, c, s1, s2)


def _rope_tables(S, head_dim):
    rot_dim = head_dim // 4
    rot_half = rot_dim // 2
    inv_freq = ROPE_THETA ** (-jnp.arange(0, rot_dim, 2, dtype=F32) / rot_dim)
    ang = jnp.arange(S).astype(F32)[:, None] * inv_freq[None, :]
    cos, sin = jnp.cos(ang), jnp.sin(ang)
    d = jnp.arange(V7X_LANES) % head_dim
    idx = d % rot_half
    lo = d < rot_half
    hi = (d >= rot_half) & (d < rot_dim)
    c = jnp.where((lo | hi)[None, :], cos[:, idx], 1.0)
    s1 = jnp.where(lo[None, :], -sin[:, idx], 0.0)
    s2 = jnp.where(hi[None, :], sin[:, idx], 0.0)
    return cos.T, sin.T, c, s1, s2


def _attn_kernel(lam_ref, qt_ref, k_ref, vt_ref, gain_ref, o_ref,
                 qbd_ref, vte_ref, sa_ref, sb_ref, acc_ref, *, tk, head_dim):
    tq = qt_ref.shape[1]
    S = k_ref.shape[0]
    hw = 2 * head_dim
    n_chunks = S // tk

    @pl.when(pl.program_id(2) == 0)
    def _():
        vte_ref[0:hw, :] = vt_ref[...]
        vte_ref[hw:, :] = jnp.ones((vte_ref.shape[0] - hw, S), vte_ref.dtype)

    qbd_ref[...] = jnp.zeros_like(qbd_ref)
    qbd_ref[0:head_dim, 0:tq] = qt_ref[0:head_dim, :]
    qbd_ref[head_dim:hw, tq:2 * tq] = qt_ref[head_dim:hw, :]
    acc_ref[...] = jnp.zeros_like(acc_ref)

    def scores(c, s_ref):
        start = pl.multiple_of(c * tk, tk)
        s = jnp.dot(k_ref[pl.ds(start, tk), :], qbd_ref[...], preferred_element_type=F32)
        s_ref[...] = s
        return jnp.max(s, axis=0, keepdims=True)

    def accumulate(c, s_ref, cmax, m):
        start = pl.multiple_of(c * tk, tk)
        v = vte_ref[:, pl.ds(start, tk)]
        m_new = jnp.maximum(m, cmax)
        alpha = jnp.exp2(m - m_new)
        p = jnp.exp2(s_ref[...] - m_new).astype(BF16)
        acc_ref[...] = alpha * acc_ref[...] + jnp.dot(v, p, preferred_element_type=F32)
        return m_new

    per_trip = next(u for u in (8, 4, 2) if n_chunks % u == 0)
    bufs = (sa_ref, sb_ref)

    def trip(c, carry, last):
        m, cmax = carry
        for i in range(per_trip):
            if not (last and i == per_trip - 1):
                cmax_next = scores(c + i + 1, bufs[(i + 1) % 2])
            m = accumulate(c + i, bufs[i % 2], cmax, m)
            cmax = cmax_next
        return m, cmax

    n_trips = n_chunks // per_trip
    neg = jnp.full((1, 2 * tq), -jnp.inf, F32)
    carry = lax.fori_loop(0, n_trips - 1, lambda j, carry: trip(per_trip * j, carry, False),
                          (neg, scores(0, sa_ref)))
    trip((n_trips - 1) * per_trip, carry, True)

    lq1, lk1, lq2, lk2 = (lam_ref[i:i + 1, :] for i in range(4))
    lam = (jnp.exp(jnp.sum(lq1 * lk1, axis=-1, keepdims=True))
           - jnp.exp(jnp.sum(lq2 * lk2, axis=-1, keepdims=True)) + LAMBDA_INIT)
    o = (acc_ref[0:hw, 0:tq] / acc_ref[hw:hw + 1, 0:tq]
         - lam * (acc_ref[0:hw, tq:2 * tq] / acc_ref[hw:hw + 1, tq:2 * tq]))
    ms = jnp.mean(o * o, axis=0, keepdims=True)
    o = o * lax.rsqrt(ms + EPS) * gain_ref[...] * (1.0 - LAMBDA_INIT)
    o_ref[...] = o.T.astype(o_ref.dtype)


ONES_ROWS = 2 * V7X_SUBLANES


def _diff_attention(qt, k, vt, lam_params, gain, head_dim):
    B, W, S = qt.shape
    hw = 2 * head_dim
    H = W // hw
    assert hw % V7X_LANES == 0, "one head's two maps must fill whole lane tiles"
    tq = _pick(S, ATTN_Q_BLOCK, V7X_LANES)
    tk = _pick(S // 2, ATTN_K_CHUNK, V7X_LANES)
    blocks = (_nbytes((hw, tq), BF16) + 2 * _nbytes((S, hw), BF16) + _nbytes((tq, hw), BF16))
    scratch = (_nbytes((hw, 2 * tq), BF16) + _nbytes((hw + ONES_ROWS, S), BF16)
               + 2 * _nbytes((tk, 2 * tq), F32) + _nbytes((hw + ONES_ROWS, 2 * tq), F32))
    temps = 2 * _nbytes((tk, 2 * tq), F32)
    return pl.pallas_call(
        functools.partial(_attn_kernel, tk=tk, head_dim=head_dim),
        grid=(B, H, S // tq),
        in_specs=[pl.BlockSpec((4, head_dim), lambda b, h, i: (0, 0)),
                  pl.BlockSpec((None, hw, tq), lambda b, h, i: (b, h, i)),
                  pl.BlockSpec((None, S, hw), lambda b, h, i: (b, 0, h)),
                  pl.BlockSpec((None, hw, S), lambda b, h, i: (b, h, 0)),
                  pl.BlockSpec((hw, 1), lambda b, h, i: (0, 0))],
        out_specs=pl.BlockSpec((None, tq, hw), lambda b, h, i: (b, i, h)),
        out_shape=jax.ShapeDtypeStruct((B, S, W), BF16),
        scratch_shapes=[pltpu.VMEM((hw, 2 * tq), BF16),
                        pltpu.VMEM((hw + ONES_ROWS, S), BF16),
                        pltpu.VMEM((tk, 2 * tq), F32),
                        pltpu.VMEM((tk, 2 * tq), F32),
                        pltpu.VMEM((hw + ONES_ROWS, 2 * tq), F32)],
        compiler_params=_params(("arbitrary", "arbitrary", "arbitrary"), blocks, scratch + temps),
        name="diff_attention",
    )(lam_params, qt, k, vt, gain.reshape(hw, 1).astype(F32))


def _softplus(x):
    return jnp.maximum(x, 0.0) + jnp.log1p(jnp.exp(-jnp.abs(x)))


def _rglru_kernel(*refs, reverse, n_chunks, conv_width):
    if reverse:
        (x_ref, xp_ref, xn_ref, cw_ref, cb_ref, w_ref, b_ref, lam_ref, hf_ref, y_ref,
         o_ref, xe_ref, a_ref, u_ref, h_ref) = refs
    else:
        (x_ref, xp_ref, xn_ref, cw_ref, cb_ref, w_ref, b_ref, lam_ref,
         o_ref, xe_ref, a_ref, u_ref, h_ref) = refs
    T = x_ref.shape[0]
    halo = xp_ref.shape[0]
    nblk, bw = w_ref.shape[0], w_ref.shape[1]
    step = pl.program_id(1)
    chunk = (n_chunks - 1 - step) if reverse else step

    xe_ref[0:halo, :] = jnp.where(chunk == 0, 0.0, xp_ref[...])
    xe_ref[halo:halo + T, :] = x_ref[...]
    xe_ref[halo + T:2 * halo + T, :] = jnp.where(chunk == n_chunks - 1, 0.0, xn_ref[...])
    left = conv_width // 2
    xc = cb_ref[...] + cw_ref[0:1, :] * xe_ref[halo - left:halo - left + T, :]
    for j in range(1, conv_width):
        xc = xc + cw_ref[j:j + 1, :] * xe_ref[halo - left + j:halo - left + j + T, :]

    sp = _softplus(-lam_ref[...])
    for n in range(nblk):
        cols = slice(n * bw, (n + 1) * bw)
        xb = xc[:, cols]
        z = jnp.dot(xb.astype(BF16), w_ref[n], preferred_element_type=F32) + b_ref[n]
        r = jax.nn.sigmoid(z[:, :bw])
        i = jax.nn.sigmoid(z[:, bw:])
        log_a = -RG_C * r * sp[:, cols]
        t = jnp.tanh(log_a)
        g = jnp.maximum(-2.0 * t / (1.0 - t), 0.0)
        gain = g * lax.rsqrt(jnp.maximum(g, jnp.finfo(F32).tiny))
        a_ref[:, cols] = jnp.exp(log_a)
        u_ref[:, cols] = gain * (i * xb)

    @pl.when(step == 0)
    def _():
        h_ref[...] = jnp.zeros_like(h_ref)

    def group(g, h):
        gi = (T // V7X_SUBLANES - 1 - g) if reverse else g
        base = pl.multiple_of(gi * V7X_SUBLANES, V7X_SUBLANES)
        order = range(V7X_SUBLANES - 1, -1, -1) if reverse else range(V7X_SUBLANES)
        for r_ in order:
            row = pl.ds(base + r_, 1)
            h = a_ref[row, :] * h + u_ref[row, :]
            u_ref[row, :] = h
        return h

    h_ref[...] = lax.fori_loop(0, T // V7X_SUBLANES, group, h_ref[...])

    if reverse:
        o_ref[...] = ((hf_ref[...] + u_ref[...]) * jax.nn.gelu(y_ref[...])).astype(o_ref.dtype)
    else:
        o_ref[...] = u_ref[...]


def _rglru_pass(rest, x_col, y_col, conv_w, conv_b, wcat, bcat, lam, hf, reverse):
    B, S, _ = rest.shape
    nblk, bw, _ = wcat.shape
    C = nblk * bw
    conv_width = conv_w.shape[0]
    T = _pick(S, RGLRU_CHUNK, V7X_SUBLANES)
    halo = V7X_SUBLANES
    assert conv_width - 1 <= halo and bw % V7X_LANES == 0
    n_chunks = S // T
    hpc = T // halo

    def cidx(c):
        return (n_chunks - 1 - c) if reverse else c

    in_specs = [
        pl.BlockSpec((None, T, C), lambda b, c: (b, cidx(c), x_col)),
        pl.BlockSpec((None, halo, C),
                     lambda b, c: (b, jnp.maximum(cidx(c) * hpc - 1, 0), x_col)),
        pl.BlockSpec((None, halo, C),
                     lambda b, c: (b, jnp.minimum((cidx(c) + 1) * hpc, S // halo - 1), x_col)),
        pl.BlockSpec((conv_width, C), lambda b, c: (0, 0)),
        pl.BlockSpec((1, C), lambda b, c: (0, 0)),
        pl.BlockSpec((nblk, bw, 2 * bw), lambda b, c: (0, 0, 0)),
        pl.BlockSpec((nblk, 1, 2 * bw), lambda b, c: (0, 0, 0)),
        pl.BlockSpec((1, C), lambda b, c: (0, 0)),
    ]
    args = [rest, rest, rest, conv_w, conv_b, wcat, bcat, lam]
    blocks = 2 * _nbytes((T, C), F32) + 2 * _nbytes((halo, C), F32) + 2 * _nbytes(wcat.shape, BF16)
    if reverse:
        in_specs += [pl.BlockSpec((None, T, C), lambda b, c: (b, cidx(c), 0)),
                     pl.BlockSpec((None, T, C), lambda b, c: (b, cidx(c), y_col))]
        args += [hf, rest]
        blocks += 2 * _nbytes((T, C), F32)
        out_dtype = BF16
    else:
        out_dtype = F32
    scratch = [pltpu.VMEM((T + 2 * halo, C), F32), pltpu.VMEM((T, C), F32),
               pltpu.VMEM((T, C), F32), pltpu.VMEM((1, C), F32)]
    return pl.pallas_call(
        functools.partial(_rglru_kernel, reverse=reverse, n_chunks=n_chunks, conv_width=conv_width),
        grid=(B, n_chunks),
        in_specs=in_specs,
        out_specs=pl.BlockSpec((None, T, C), lambda b, c: (b, cidx(c), 0)),
        out_shape=jax.ShapeDtypeStruct((B, S, C), out_dtype),
        scratch_shapes=scratch,
        compiler_params=_params(("parallel", "arbitrary"), blocks, 8 * _nbytes((T, C), F32)),
        name="rglru_bwd" if reverse else "rglru_fwd",
    )(*args)


def _merge_kernel(o_ref, r_ref, wa_ref, wr_ref, ga_ref, gr_ref, out_ref):
    a = jnp.dot(o_ref[...], wa_ref[...], preferred_element_type=F32)
    r = jnp.dot(r_ref[...], wr_ref[...], preferred_element_type=F32)
    out_ref[...] = (jax.nn.sigmoid(ga_ref[...]) * a + jax.nn.sigmoid(gr_ref[...]) * r).astype(out_ref.dtype)


def _merge(o, rr, wa, wr, rest, gate_col0):
    M, K = o.shape
    N = wa.shape[1]
    bm = _pick(M, MM_BLOCK, V7X_SUBLANES)
    bn = _pick(N, MERGE_COL_BLOCK, V7X_LANES)
    nb = N // bn
    ga0 = gate_col0 // bn
    blocks = (2 * _nbytes((bm, K), BF16) + 2 * _nbytes((K, bn), BF16) + 2 * _nbytes((bm, bn), F32)
              + _nbytes((bm, bn), BF16))
    return pl.pallas_call(
        _merge_kernel,
        grid=(M // bm, nb),
        in_specs=[pl.BlockSpec((bm, K), lambda i, j: (i, 0)),
                  pl.BlockSpec((bm, K), lambda i, j: (i, 0)),
                  pl.BlockSpec((K, bn), lambda i, j: (0, j)),
                  pl.BlockSpec((K, bn), lambda i, j: (0, j)),
                  pl.BlockSpec((bm, bn), lambda i, j: (i, ga0 + j)),
                  pl.BlockSpec((bm, bn), lambda i, j: (i, ga0 + nb + j))],
        out_specs=pl.BlockSpec((bm, bn), lambda i, j: (i, j)),
        out_shape=jax.ShapeDtypeStruct((M, N), BF16),
        compiler_params=_params(("parallel", "arbitrary"), blocks, 3 * _nbytes((bm, bn), F32)),
        name="merge",
    )(o, rr, wa, wr, rest, rest)


def _prepare_weights(ffn1_w_in, ffn1_w_out, w_in, conv_w, conv_b, rg_wa, rg_ba, rg_wx, rg_bx, rg_lambda,
                     w_attn_proj, w_rnn_proj, w_out, ffn2_w_in, ffn2_w_out, qk_width, rnn_width):
    def ffn_w(w_i, w_o):
        return w_i.astype(BF16), w_o.astype(BF16)

    c1, c2, c3 = qk_width, 2 * qk_width, 3 * qk_width
    wi = w_in.astype(BF16)
    nblk, bw = rg_wa.shape[1], rg_wa.shape[2]
    return dict(
        ffn1=ffn_w(ffn1_w_in[0], ffn1_w_out[0]),
        ffn2=ffn_w(ffn2_w_in[0], ffn2_w_out[0]),
        w_in=wi, wq_t=wi[:, :c1].T, wv_t=wi[:, c2:c3].T,
        conv_w=conv_w.astype(F32), conv_b=conv_b.reshape(1, rnn_width).astype(F32),
        wcat=jnp.concatenate([rg_wa, rg_wx], axis=-1).astype(BF16),
        bcat=jnp.concatenate([rg_ba, rg_bx], axis=-1).reshape(2, nblk, 1, 2 * bw).astype(F32),
        lam=rg_lambda.reshape(2, 1, rnn_width).astype(F32),
        w_attn=w_attn_proj.astype(BF16), w_rnn=w_rnn_proj.astype(BF16), w_out=w_out.astype(BF16),
    )


def _trunk(x, p, ffn1_norm, mix_norm, lam_params, subln_gain, ffn2_norm, final_norm, head_dim, rnn_width):
    B, S, D = x.shape
    M = B * S
    x = x.reshape(M, D)
    x = _ffn(x, ffn1_norm, *p["ffn1"])

    h = _rmsnorm(x, mix_norm, BF16)
    h3 = h.reshape(B, S, D)
    cos_t, sin_t, c, s1, s2 = _rope_tables(S, head_dim)
    qt = _proj_t(h3, p["wq_t"], rope=(cos_t, sin_t, head_dim, head_dim ** -0.5 * math.log2(math.e)))
    qk_width = p["wq_t"].shape[0]
    k = _proj_k(h3, p["w_in"], qk_width, qk_width, c, s1, s2, cos_t.shape[0])
    vt = _proj_t(h3, p["wv_t"])
    c3 = 3 * qk_width
    rest = _mm(h, p["w_in"], c3, p["w_in"].shape[1] - c3, F32)

    o = _diff_attention(qt, k, vt, lam_params, subln_gain, head_dim)

    rest3 = rest.reshape(B, S, rest.shape[1])
    rnn = (rest3, 0, 1, p["conv_w"], p["conv_b"])
    hf = _rglru_pass(*rnn, p["wcat"][0], p["bcat"][0], p["lam"][0], None, reverse=False)
    rr = _rglru_pass(*rnn, p["wcat"][1], p["bcat"][1], p["lam"][1], hf, reverse=True)

    merged = _merge(o.reshape(M, -1), rr.reshape(M, rnn_width), p["w_attn"], p["w_rnn"], rest, 2 * rnn_width)
    x = _mm_residual(merged, p["w_out"], x, 1.0)

    x = _ffn(x, ffn2_norm, *p["ffn2"])
    return _rmsnorm(x, final_norm, F32).reshape(B, S, D)


def kernel(x_prompt, x_sample, ffn1_norm, ffn1_w_in, ffn1_w_out, mix_norm, w_in, lambda_q1, lambda_k1, lambda_q2, lambda_k2, subln_gain, conv_w, conv_b, rg_wa, rg_ba, rg_wx, rg_bx, rg_lambda, w_attn_proj, w_rnn_proj, w_out, ffn2_norm, ffn2_w_in, ffn2_w_out, final_norm):
    assert ffn1_norm.shape[0] == 1, "single-layer trunk"
    head_dim = lambda_q1.shape[-1]
    rnn_width = conv_w.shape[-1]
    D = x_prompt.shape[-1]
    qk_width = (w_in.shape[-1] - 2 * rnn_width - 2 * D) // 3
    p = _prepare_weights(ffn1_w_in, ffn1_w_out, w_in[0], conv_w[0], conv_b[0], rg_wa[0], rg_ba[0],
                         rg_wx[0], rg_bx[0], rg_lambda[0], w_attn_proj[0], w_rnn_proj[0], w_out[0],
                         ffn2_w_in, ffn2_w_out, qk_width, rnn_width)
    lam_params = jnp.concatenate([lambda_q1, lambda_k1, lambda_q2, lambda_k2], axis=0).astype(F32)
    args = (p, ffn1_norm[0], mix_norm[0], lam_params, subln_gain[0], ffn2_norm[0], final_norm,
            head_dim, rnn_width)
    return _trunk(x_prompt, *args), _trunk(x_sample, *args)
```

```python
import functools
import math

import jax
import jax.numpy as jnp
from jax import lax
from jax.experimental import pallas as pl
from jax.experimental.pallas import tpu as pltpu

EPS = 1e-6
ROPE_THETA = 500000.0
RG_C = 8.0
LAMBDA_INIT = 0.8 - 0.6 * math.exp(-0.3 * 0)

V7X_LANES = 128
V7X_SUBLANES = 8
V7X_MXU = 256
V7X_VMEM_BYTES = 64 * 1024 * 1024
VMEM_CAP = V7X_VMEM_BYTES - 6 * 1024 * 1024

MM_BLOCK = 1024
MM_BLOCK_SMALL = 512
MM_K_BLOCK = 3072
FFN_HIDDEN_BLOCK = 512
MERGE_COL_BLOCK = 512
NORM_ROWS = 512
ATTN_Q_BLOCK = 1024
ATTN_K_CHUNK = 512
RGLRU_CHUNK = 256

F32 = jnp.float32
BF16 = jnp.bfloat16


def _pick(n, target, align):
    best = None
    for d in range(align, min(n, target) + 1, align):
        if n % d == 0:
            best = d
    return best if best is not None else n


def _nbytes(shape, dtype):
    return math.prod(shape) * jnp.dtype(dtype).itemsize


def _params(semantics, pipelined_bytes, resident_bytes=0):
    need = 2 * pipelined_bytes + resident_bytes + 4 * 1024 * 1024
    return pltpu.CompilerParams(dimension_semantics=semantics,
                                vmem_limit_bytes=int(min(max(need, 16 * 1024 * 1024), VMEM_CAP)))


def _rmsnorm_kernel(x_ref, g_ref, o_ref):
    x = x_ref[...]
    ms = jnp.mean(x * x, axis=-1, keepdims=True)
    o_ref[...] = (x * lax.rsqrt(ms + EPS) * g_ref[...]).astype(o_ref.dtype)


def _rmsnorm(x, g, out_dtype):
    M, D = x.shape
    bm = _pick(M, NORM_ROWS, V7X_SUBLANES)
    blocks = _nbytes((bm, D), F32) + _nbytes((bm, D), out_dtype)
    return pl.pallas_call(
        _rmsnorm_kernel,
        grid=(M // bm,),
        in_specs=[pl.BlockSpec((bm, D), lambda i: (i, 0)),
                  pl.BlockSpec((1, D), lambda i: (0, 0))],
        out_specs=pl.BlockSpec((bm, D), lambda i: (i, 0)),
        out_shape=jax.ShapeDtypeStruct((M, D), out_dtype),
        compiler_params=_params(("parallel",), blocks, _nbytes((bm, D), F32)),
        name="rmsnorm",
    )(x, g.reshape(1, D).astype(F32))


def _ffn_up_kernel(h_ref, wg_ref, wu_ref, o_ref):
    h = h_ref[...]
    g = jnp.dot(h, wg_ref[...], preferred_element_type=F32)
    u = jnp.dot(h, wu_ref[...], preferred_element_type=F32)
    o_ref[...] = (g * jax.nn.sigmoid(g) * u).astype(o_ref.dtype)


def _ffn_up(h, w_gu):
    M, D = h.shape
    Fp = w_gu.shape[1] // 2
    bf = _pick(Fp, FFN_HIDDEN_BLOCK, V7X_LANES)
    bm = _pick(M, MM_BLOCK * FFN_HIDDEN_BLOCK // bf, V7X_SUBLANES)
    nf = Fp // bf
    blocks = _nbytes((bm, D), BF16) + 2 * _nbytes((D, bf), BF16) + _nbytes((bm, bf), BF16)
    return pl.pallas_call(
        _ffn_up_kernel,
        grid=(M // bm, nf),
        in_specs=[pl.BlockSpec((bm, D), lambda i, j: (i, 0)),
                  pl.BlockSpec((D, bf), lambda i, j: (0, j)),
                  pl.BlockSpec((D, bf), lambda i, j: (0, nf + j))],
        out_specs=pl.BlockSpec((bm, bf), lambda i, j: (i, j)),
        out_shape=jax.ShapeDtypeStruct((M, Fp), BF16),
        compiler_params=_params(("parallel", "arbitrary"), blocks, 3 * _nbytes((bm, bf), F32)),
        name="ffn_up",
    )(h, w_gu, w_gu)


def _mm_res_kernel(a_ref, w_ref, r_ref, o_ref, *, scale):
    part = scale * jnp.dot(a_ref[...], w_ref[...], preferred_element_type=F32)

    @pl.when(pl.program_id(2) == 0)
    def _():
        o_ref[...] = r_ref[...] + part

    @pl.when(pl.program_id(2) != 0)
    def _():
        o_ref[...] += part


def _mm_residual(a, w, res, scale):
    M, K = a.shape
    N = w.shape[1]
    def footprint(bm, bn, bk):
        return _nbytes((bm, bk), BF16) + _nbytes((bk, bn), BF16) + 2 * _nbytes((bm, bn), F32)

    for t in (MM_BLOCK, MM_BLOCK_SMALL):
        bm, bn, bk = _pick(M, t, V7X_SUBLANES), _pick(N, t, V7X_LANES), K
        if 2 * footprint(bm, bn, bk) <= VMEM_CAP - 6 * 1024 * 1024:
            break
    else:
        bm, bn = _pick(M, MM_BLOCK, V7X_SUBLANES), _pick(N, MM_BLOCK, V7X_LANES)
        bk = _pick(K, MM_K_BLOCK, V7X_MXU)
    blocks = footprint(bm, bn, bk)
    return pl.pallas_call(
        functools.partial(_mm_res_kernel, scale=scale),
        grid=(M // bm, N // bn, K // bk),
        in_specs=[pl.BlockSpec((bm, bk), lambda i, j, k: (i, k)),
                  pl.BlockSpec((bk, bn), lambda i, j, k: (k, j)),
                  pl.BlockSpec((bm, bn), lambda i, j, k: (i, j))],
        out_specs=pl.BlockSpec((bm, bn), lambda i, j, k: (i, j)),
        out_shape=jax.ShapeDtypeStruct((M, N), F32),
        compiler_params=_params(("parallel", "parallel", "arbitrary"), blocks,
                                2 * _nbytes((bm, bn), F32)),
        name="mm_residual",
    )(a, w, res)


def _ffn(x, norm_g, w_gu, wo):
    h = _rmsnorm(x, norm_g, BF16)
    a = _ffn_up(h, w_gu)
    return _mm_residual(a, wo, x, 0.5)


def _mm_kernel(x_ref, w_ref, o_ref):
    o_ref[...] = jnp.dot(x_ref[...], w_ref[...], preferred_element_type=F32).astype(o_ref.dtype)


def _mm(x, w, col0, N, out_dtype):
    M, K = x.shape
    bm = _pick(M, MM_BLOCK, V7X_SUBLANES)
    bn = _pick(math.gcd(N, col0) if col0 else N, MM_BLOCK, V7X_LANES)
    j0 = col0 // bn
    blocks = _nbytes((bm, K), BF16) + _nbytes((K, bn), BF16) + _nbytes((bm, bn), out_dtype)
    return pl.pallas_call(
        _mm_kernel,
        grid=(M // bm, N // bn),
        in_specs=[pl.BlockSpec((bm, K), lambda i, j: (i, 0)),
                  pl.BlockSpec((K, bn), lambda i, j: (0, j0 + j))],
        out_specs=pl.BlockSpec((bm, bn), lambda i, j: (i, j)),
        out_shape=jax.ShapeDtypeStruct((M, N), out_dtype),
        compiler_params=_params(("parallel", "arbitrary"), blocks, _nbytes((bm, bn), F32)),
        name="mm",
    )(x, w)


def _nt_dot(w, h):
    return lax.dot_general(w, h, (((1,), (1,)), ((), ())), preferred_element_type=F32)


def _proj_vt_kernel(w_ref, h_ref, o_ref):
    o_ref[...] = _nt_dot(w_ref[...], h_ref[...]).astype(o_ref.dtype)


def _proj_qt_kernel(w_ref, h_ref, cos_ref, sin_ref, o_ref, *, head_dim, rot_half, scale):
    acc = _nt_dot(w_ref[...], h_ref[...])
    cos = cos_ref[...]
    sin = sin_ref[...]
    for g in range(acc.shape[0] // head_dim):
        base = g * head_dim
        x1 = acc[base:base + rot_half]
        x2 = acc[base + rot_half:base + 2 * rot_half]
        rest = acc[base + 2 * rot_half:base + head_dim]
        rot = jnp.concatenate([x1 * cos - x2 * sin, x2 * cos + x1 * sin, rest], axis=0)
        o_ref[base:base + head_dim, :] = (rot * scale).astype(o_ref.dtype)


def _proj_t(h3, wt, rope=None):
    B, S, K = h3.shape
    N = wt.shape[0]
    bm = _pick(S, MM_BLOCK, V7X_LANES)
    bn = _pick(N, MM_BLOCK, V7X_LANES)
    blocks = _nbytes((bm, K), BF16) + _nbytes((bn, K), BF16) + _nbytes((bn, bm), BF16)
    in_specs = [pl.BlockSpec((bn, K), lambda b, i, j: (j, 0)),
                pl.BlockSpec((None, bm, K), lambda b, i, j: (b, i, 0))]
    args = [wt, h3]
    if rope is None:
        body = _proj_vt_kernel
    else:
        cos_t, sin_t, head_dim, scale = rope
        rot_half = cos_t.shape[0]
        body = functools.partial(_proj_qt_kernel, head_dim=head_dim, rot_half=rot_half, scale=scale)
        in_specs += [pl.BlockSpec((rot_half, bm), lambda b, i, j: (0, i)),
                     pl.BlockSpec((rot_half, bm), lambda b, i, j: (0, i))]
        args += [cos_t, sin_t]
    return pl.pallas_call(
        body,
        grid=(B, S // bm, N // bn),
        in_specs=in_specs,
        out_specs=pl.BlockSpec((None, bn, bm), lambda b, i, j: (b, j, i)),
        out_shape=jax.ShapeDtypeStruct((B, N, S), BF16),
        compiler_params=_params(("parallel", "parallel", "arbitrary"), blocks,
                                2 * _nbytes((bn, bm), F32)),
        name="proj_t",
    )(*args)


def _proj_k_kernel(h_ref, w_ref, c_ref, s1_ref, s2_ref, o_ref, *, rot_half):
    acc = jnp.dot(h_ref[...], w_ref[...], preferred_element_type=F32)
    c = c_ref[...]
    s1 = s1_ref[...]
    s2 = s2_ref[...]
    for t in range(acc.shape[1] // V7X_LANES):
        x = acc[:, t * V7X_LANES:(t + 1) * V7X_LANES]
        up = pltpu.roll(x, V7X_LANES - rot_half, 1)
        dn = pltpu.roll(x, rot_half, 1)
        o_ref[:, t * V7X_LANES:(t + 1) * V7X_LANES] = (x * c + up * s1 + dn * s2).astype(o_ref.dtype)


def _proj_k(h3, w, col0, N, c, s1, s2, rot_half):
    B, S, K = h3.shape
    bm = _pick(S, MM_BLOCK, V7X_SUBLANES)
    bn = _pick(math.gcd(N, col0) if col0 else N, MM_BLOCK, V7X_LANES)
    j0 = col0 // bn
    blocks = (_nbytes((bm, K), BF16) + _nbytes((K, bn), BF16) + _nbytes((bm, bn), BF16)
              + 3 * _nbytes((bm, V7X_LANES), F32))
    tab = pl.BlockSpec((bm, V7X_LANES), lambda b, i, j: (i, 0))
    return pl.pallas_call(
        functools.partial(_proj_k_kernel, rot_half=rot_half),
        grid=(B, S // bm, N // bn),
        in_specs=[pl.BlockSpec((None, bm, K), lambda b, i, j: (b, i, 0)),
                  pl.BlockSpec((K, bn), lambda b, i, j: (0, j0 + j)),
                  tab, tab, tab],
        out_specs=pl.BlockSpec((None, bm, bn), lambda b, i, j: (b, i, j)),
        out_shape=jax.ShapeDtypeStruct((B, S, N), BF16),
        compiler_params=_params(("parallel", "parallel", "arbitrary"), blocks,
                                2 * _nbytes((bm, bn), F32)),
        name="proj_k",
    )(h3, w, c, s1, s2)


def _rope_tables(S, head_dim):
    rot_dim = head_dim // 4
    rot_half = rot_dim // 2
    inv_freq = ROPE_THETA ** (-jnp.arange(0, rot_dim, 2, dtype=F32) / rot_dim)
    ang = jnp.arange(S).astype(F32)[:, None] * inv_freq[None, :]
    cos, sin = jnp.cos(ang), jnp.sin(ang)
    d = jnp.arange(V7X_LANES) % head_dim
    idx = d % rot_half
    lo = d < rot_half
    hi = (d >= rot_half) & (d < rot_dim)
    c = jnp.where((lo | hi)[None, :], cos[:, idx], 1.0)
    s1 = jnp.where(lo[None, :], -sin[:, idx], 0.0)
    s2 = jnp.where(hi[None, :], sin[:, idx], 0.0)
    return cos.T, sin.T, c, s1, s2


def _attn_kernel(lam_ref, qt_ref, k_ref, vt_ref, gain_ref, o_ref,
                 qbd_ref, vte_ref, sa_ref, sb_ref, acc_ref, *, tk, head_dim):
    tq = qt_ref.shape[1]
    S = k_ref.shape[0]
    hw = 2 * head_dim
    n_chunks = S // tk

    @pl.when(pl.program_id(2) == 0)
    def _():
        vte_ref[0:hw, :] = vt_ref[...]
        vte_ref[hw:, :] = jnp.ones((vte_ref.shape[0] - hw, S), vte_ref.dtype)

    qbd_ref[...] = jnp.zeros_like(qbd_ref)
    qbd_ref[0:head_dim, 0:tq] = qt_ref[0:head_dim, :]
    qbd_ref[head_dim:hw, tq:2 * tq] = qt_ref[head_dim:hw, :]
    acc_ref[...] = jnp.zeros_like(acc_ref)

    def scores(c, s_ref):
        start = pl.multiple_of(c * tk, tk)
        s = jnp.dot(k_ref[pl.ds(start, tk), :], qbd_ref[...], preferred_element_type=F32)
        s_ref[...] = s
        return jnp.max(s, axis=0, keepdims=True)

    def accumulate(c, s_ref, cmax, m):
        start = pl.multiple_of(c * tk, tk)
        v = vte_ref[:, pl.ds(start, tk)]
        m_new = jnp.maximum(m, cmax)
        alpha = jnp.exp2(m - m_new)
        p = jnp.exp2(s_ref[...] - m_new).astype(BF16)
        acc_ref[...] = alpha * acc_ref[...] + jnp.dot(v, p, preferred_element_type=F32)
        return m_new

    per_trip = next(u for u in (8, 4, 2) if n_chunks % u == 0)
    bufs = (sa_ref, sb_ref)

    def trip(c, carry, last):
        m, cmax = carry
        for i in range(per_trip):
            if not (last and i == per_trip - 1):
                cmax_next = scores(c + i + 1, bufs[(i + 1) % 2])
            m = accumulate(c + i, bufs[i % 2], cmax, m)
            cmax = cmax_next
        return m, cmax

    n_trips = n_chunks // per_trip
    neg = jnp.full((1, 2 * tq), -jnp.inf, F32)
    carry = lax.fori_loop(0, n_trips - 1, lambda j, carry: trip(per_trip * j, carry, False),
                          (neg, scores(0, sa_ref)))
    trip((n_trips - 1) * per_trip, carry, True)

    lq1, lk1, lq2, lk2 = (lam_ref[i:i + 1, :] for i in range(4))
    lam = (jnp.exp(jnp.sum(lq1 * lk1, axis=-1, keepdims=True))
           - jnp.exp(jnp.sum(lq2 * lk2, axis=-1, keepdims=True)) + LAMBDA_INIT)
    o = (acc_ref[0:hw, 0:tq] / acc_ref[hw:hw + 1, 0:tq]
         - lam * (acc_ref[0:hw, tq:2 * tq] / acc_ref[hw:hw + 1, tq:2 * tq]))
    ms = jnp.mean(o * o, axis=0, keepdims=True)
    o = o * lax.rsqrt(ms + EPS) * gain_ref[...] * (1.0 - LAMBDA_INIT)
    o_ref[...] = o.astype(o_ref.dtype)


ONES_ROWS = 2 * V7X_SUBLANES


def _diff_attention(qt, k, vt, lam_params, gain, head_dim):
    B, W, S = qt.shape
    hw = 2 * head_dim
    H = W // hw
    assert hw % V7X_LANES == 0, "one head's two maps must fill whole lane tiles"
    tq = _pick(S, ATTN_Q_BLOCK, V7X_LANES)
    tk = _pick(S // 2, ATTN_K_CHUNK, V7X_LANES)
    blocks = (_nbytes((hw, tq), BF16) + 2 * _nbytes((S, hw), BF16) + _nbytes((tq, hw), BF16))
    scratch = (_nbytes((hw, 2 * tq), BF16) + _nbytes((hw + ONES_ROWS, S), BF16)
               + 2 * _nbytes((tk, 2 * tq), F32) + _nbytes((hw + ONES_ROWS, 2 * tq), F32))
    temps = 2 * _nbytes((tk, 2 * tq), F32)
    return pl.pallas_call(
        functools.partial(_attn_kernel, tk=tk, head_dim=head_dim),
        grid=(B, H, S // tq),
        in_specs=[pl.BlockSpec((4, head_dim), lambda b, h, i: (0, 0)),
                  pl.BlockSpec((None, hw, tq), lambda b, h, i: (b, h, i)),
                  pl.BlockSpec((None, S, hw), lambda b, h, i: (b, 0, h)),
                  pl.BlockSpec((None, hw, S), lambda b, h, i: (b, h, 0)),
                  pl.BlockSpec((hw, 1), lambda b, h, i: (0, 0))],
        out_specs=pl.BlockSpec((None, hw, tq), lambda b, h, i: (b, h, i)),
        out_shape=jax.ShapeDtypeStruct((B, W, S), BF16),
        scratch_shapes=[pltpu.VMEM((hw, 2 * tq), BF16),
                        pltpu.VMEM((hw + ONES_ROWS, S), BF16),
                        pltpu.VMEM((tk, 2 * tq), F32),
                        pltpu.VMEM((tk, 2 * tq), F32),
                        pltpu.VMEM((hw + ONES_ROWS, 2 * tq), F32)],
        compiler_params=_params(("arbitrary", "arbitrary", "arbitrary"), blocks, scratch + temps),
        name="diff_attention",
    )(lam_params, qt, k, vt, gain.reshape(hw, 1).astype(F32))


def _softplus(x):
    return jnp.maximum(x, 0.0) + jnp.log1p(jnp.exp(-jnp.abs(x)))


def _rglru_kernel(*refs, reverse, n_chunks, conv_width):
    if reverse:
        (x_ref, xp_ref, xn_ref, cw_ref, cb_ref, w_ref, b_ref, lam_ref, hf_ref, y_ref,
         o_ref, xe_ref, a_ref, u_ref, h_ref) = refs
    else:
        (x_ref, xp_ref, xn_ref, cw_ref, cb_ref, w_ref, b_ref, lam_ref,
         o_ref, xe_ref, a_ref, u_ref, h_ref) = refs
    T = x_ref.shape[0]
    halo = xp_ref.shape[0]
    nblk, bw = w_ref.shape[0], w_ref.shape[1]
    step = pl.program_id(1)
    chunk = (n_chunks - 1 - step) if reverse else step

    xe_ref[0:halo, :] = jnp.where(chunk == 0, 0.0, xp_ref[...])
    xe_ref[halo:halo + T, :] = x_ref[...]
    xe_ref[halo + T:2 * halo + T, :] = jnp.where(chunk == n_chunks - 1, 0.0, xn_ref[...])
    left = conv_width // 2
    xc = cb_ref[...] + cw_ref[0:1, :] * xe_ref[halo - left:halo - left + T, :]
    for j in range(1, conv_width):
        xc = xc + cw_ref[j:j + 1, :] * xe_ref[halo - left + j:halo - left + j + T, :]

    sp = _softplus(-lam_ref[...])
    for n in range(nblk):
        cols = slice(n * bw, (n + 1) * bw)
        xb = xc[:, cols]
        z = jnp.dot(xb.astype(BF16), w_ref[n], preferred_element_type=F32) + b_ref[n]
        r = jax.nn.sigmoid(z[:, :bw])
        i = jax.nn.sigmoid(z[:, bw:])
        log_a = -RG_C * r * sp[:, cols]
        t = jnp.tanh(log_a)
        g = jnp.maximum(-2.0 * t / (1.0 - t), 0.0)
        gain = g * lax.rsqrt(jnp.maximum(g, jnp.finfo(F32).tiny))
        a_ref[:, cols] = jnp.exp(log_a)
        u_ref[:, cols] = gain * (i * xb)

    @pl.when(step == 0)
    def _():
        h_ref[...] = jnp.zeros_like(h_ref)

    def group(g, h):
        gi = (T // V7X_SUBLANES - 1 - g) if reverse else g
        base = pl.multiple_of(gi * V7X_SUBLANES, V7X_SUBLANES)
        order = range(V7X_SUBLANES - 1, -1, -1) if reverse else range(V7X_SUBLANES)
        for r_ in order:
            row = pl.ds(base + r_, 1)
            h = a_ref[row, :] * h + u_ref[row, :]
            u_ref[row, :] = h
        return h

    h_ref[...] = lax.fori_loop(0, T // V7X_SUBLANES, group, h_ref[...])

    if reverse:
        o_ref[...] = ((hf_ref[...] + u_ref[...]) * jax.nn.gelu(y_ref[...])).astype(o_ref.dtype)
    else:
        o_ref[...] = u_ref[...]


def _rglru_pass(rest, x_col, y_col, conv_w, conv_b, wcat, bcat, lam, hf, reverse):
    B, S, _ = rest.shape
    nblk, bw, _ = wcat.shape
    C = nblk * bw
    conv_width = conv_w.shape[0]
    T = _pick(S, RGLRU_CHUNK, V7X_SUBLANES)
    halo = V7X_SUBLANES
    assert conv_width - 1 <= halo and bw % V7X_LANES == 0
    n_chunks = S // T
    hpc = T // halo

    def cidx(c):
        return (n_chunks - 1 - c) if reverse else c

    in_specs = [
        pl.BlockSpec((None, T, C), lambda b, c: (b, cidx(c), x_col)),
        pl.BlockSpec((None, halo, C),
                     lambda b, c: (b, jnp.maximum(cidx(c) * hpc - 1, 0), x_col)),
        pl.BlockSpec((None, halo, C),
                     lambda b, c: (b, jnp.minimum((cidx(c) + 1) * hpc, S // halo - 1), x_col)),
        pl.BlockSpec((conv_width, C), lambda b, c: (0, 0)),
        pl.BlockSpec((1, C), lambda b, c: (0, 0)),
        pl.BlockSpec((nblk, bw, 2 * bw), lambda b, c: (0, 0, 0)),
        pl.BlockSpec((nblk, 1, 2 * bw), lambda b, c: (0, 0, 0)),
        pl.BlockSpec((1, C), lambda b, c: (0, 0)),
    ]
    args = [rest, rest, rest, conv_w, conv_b, wcat, bcat, lam]
    blocks = 2 * _nbytes((T, C), F32) + 2 * _nbytes((halo, C), F32) + 2 * _nbytes(wcat.shape, BF16)
    if reverse:
        in_specs += [pl.BlockSpec((None, T, C), lambda b, c: (b, cidx(c), 0)),
                     pl.BlockSpec((None, T, C), lambda b, c: (b, cidx(c), y_col))]
        args += [hf, rest]
        blocks += 2 * _nbytes((T, C), F32)
        out_dtype = BF16
    else:
        out_dtype = F32
    scratch = [pltpu.VMEM((T + 2 * halo, C), F32), pltpu.VMEM((T, C), F32),
               pltpu.VMEM((T, C), F32), pltpu.VMEM((1, C), F32)]
    return pl.pallas_call(
        functools.partial(_rglru_kernel, reverse=reverse, n_chunks=n_chunks, conv_width=conv_width),
        grid=(B, n_chunks),
        in_specs=in_specs,
        out_specs=pl.BlockSpec((None, T, C), lambda b, c: (b, cidx(c), 0)),
        out_shape=jax.ShapeDtypeStruct((B, S, C), out_dtype),
        scratch_shapes=scratch,
        compiler_params=_params(("parallel", "arbitrary"), blocks, 8 * _nbytes((T, C), F32)),
        name="rglru_bwd" if reverse else "rglru_fwd",
    )(*args)


def _merge_kernel(ot_ref, r_ref, wa_ref, wr_ref, ga_ref, gr_ref, out_ref):
    a = lax.dot_general(ot_ref[...], wa_ref[...], (((0,), (0,)), ((), ())), preferred_element_type=F32)
    r = jnp.dot(r_ref[...], wr_ref[...], preferred_element_type=F32)
    out_ref[...] = (jax.nn.sigmoid(ga_ref[...]) * a + jax.nn.sigmoid(gr_ref[...]) * r).astype(out_ref.dtype)


def _merge(o_t, rr, wa, wr, rest, gate_col0):
    B, K, S = o_t.shape
    M = B * S
    N = wa.shape[1]
    bm = _pick(S, MM_BLOCK, V7X_LANES)
    spb = S // bm
    bn = _pick(N, MERGE_COL_BLOCK, V7X_LANES)
    nb = N // bn
    ga0 = gate_col0 // bn
    blocks = (2 * _nbytes((bm, K), BF16) + 2 * _nbytes((K, bn), BF16) + 2 * _nbytes((bm, bn), F32)
              + _nbytes((bm, bn), BF16))
    return pl.pallas_call(
        _merge_kernel,
        grid=(M // bm, nb),
        in_specs=[pl.BlockSpec((None, K, bm), lambda i, j: (i // spb, 0, i % spb)),
                  pl.BlockSpec((bm, K), lambda i, j: (i, 0)),
                  pl.BlockSpec((K, bn), lambda i, j: (0, j)),
                  pl.BlockSpec((K, bn), lambda i, j: (0, j)),
                  pl.BlockSpec((bm, bn), lambda i, j: (i, ga0 + j)),
                  pl.BlockSpec((bm, bn), lambda i, j: (i, ga0 + nb + j))],
        out_specs=pl.BlockSpec((bm, bn), lambda i, j: (i, j)),
        out_shape=jax.ShapeDtypeStruct((M, N), BF16),
        compiler_params=_params(("parallel", "arbitrary"), blocks, 3 * _nbytes((bm, bn), F32)),
        name="merge",
    )(o_t, rr, wa, wr, rest, rest)


def _prepare_weights(ffn1_w_in, ffn1_w_out, w_in, conv_w, conv_b, rg_wa, rg_ba, rg_wx, rg_bx, rg_lambda,
                     w_attn_proj, w_rnn_proj, w_out, ffn2_w_in, ffn2_w_out, qk_width, rnn_width):
    def ffn_w(w_i, w_o):
        return w_i.astype(BF16), w_o.astype(BF16)

    c1, c2, c3 = qk_width, 2 * qk_width, 3 * qk_width
    wi = w_in.astype(BF16)
    nblk, bw = rg_wa.shape[1], rg_wa.shape[2]
    return dict(
        ffn1=ffn_w(ffn1_w_in[0], ffn1_w_out[0]),
        ffn2=ffn_w(ffn2_w_in[0], ffn2_w_out[0]),
        w_in=wi, wq_t=wi[:, :c1].T, wv_t=wi[:, c2:c3].T,
        conv_w=conv_w.astype(F32), conv_b=conv_b.reshape(1, rnn_width).astype(F32),
        wcat=jnp.concatenate([rg_wa, rg_wx], axis=-1).astype(BF16),
        bcat=jnp.concatenate([rg_ba, rg_bx], axis=-1).reshape(2, nblk, 1, 2 * bw).astype(F32),
        lam=rg_lambda.reshape(2, 1, rnn_width).astype(F32),
        w_attn=w_attn_proj.astype(BF16), w_rnn=w_rnn_proj.astype(BF16), w_out=w_out.astype(BF16),
    )


def _trunk(x, p, ffn1_norm, mix_norm, lam_params, subln_gain, ffn2_norm, final_norm, head_dim, rnn_width):
    B, S, D = x.shape
    M = B * S
    x = x.reshape(M, D)
    x = _ffn(x, ffn1_norm, *p["ffn1"])

    h = _rmsnorm(x, mix_norm, BF16)
    h3 = h.reshape(B, S, D)
    cos_t, sin_t, c, s1, s2 = _rope_tables(S, head_dim)
    qt = _proj_t(h3, p["wq_t"], rope=(cos_t, sin_t, head_dim, head_dim ** -0.5 * math.log2(math.e)))
    qk_width = p["wq_t"].shape[0]
    k = _proj_k(h3, p["w_in"], qk_width, qk_width, c, s1, s2, cos_t.shape[0])
    vt = _proj_t(h3, p["wv_t"])
    c3 = 3 * qk_width
    rest = _mm(h, p["w_in"], c3, p["w_in"].shape[1] - c3, F32)

    o = _diff_attention(qt, k, vt, lam_params, subln_gain, head_dim)

    rest3 = rest.reshape(B, S, rest.shape[1])
    rnn = (rest3, 0, 1, p["conv_w"], p["conv_b"])
    hf = _rglru_pass(*rnn, p["wcat"][0], p["bcat"][0], p["lam"][0], None, reverse=False)
    rr = _rglru_pass(*rnn, p["wcat"][1], p["bcat"][1], p["lam"][1], hf, reverse=True)

    merged = _merge(o, rr.reshape(M, rnn_width), p["w_attn"], p["w_rnn"], rest, 2 * rnn_width)
    x = _mm_residual(merged, p["w_out"], x, 1.0)

    x = _ffn(x, ffn2_norm, *p["ffn2"])
    return _rmsnorm(x, final_norm, F32).reshape(B, S, D)


def kernel(x_prompt, x_sample, ffn1_norm, ffn1_w_in, ffn1_w_out, mix_norm, w_in, lambda_q1, lambda_k1, lambda_q2, lambda_k2, subln_gain, conv_w, conv_b, rg_wa, rg_ba, rg_wx, rg_bx, rg_lambda, w_attn_proj, w_rnn_proj, w_out, ffn2_norm, ffn2_w_in, ffn2_w_out, final_norm):
    assert ffn1_norm.shape[0] == 1, "single-layer trunk"
    head_dim = lambda_q1.shape[-1]
    rnn_width = conv_w.shape[-1]
    D = x_prompt.shape[-1]
    qk_width = (w_in.shape[-1] - 2 * rnn_width - 2 * D) // 3
    p = _prepare_weights(ffn1_w_in, ffn1_w_out, w_in[0], conv_w[0], conv_b[0], rg_wa[0], rg_ba[0],
                         rg_wx[0], rg_bx[0], rg_lambda[0], w_attn_proj[0], w_rnn_proj[0], w_out[0],
                         ffn2_w_in, ffn2_w_out, qk_width, rnn_width)
    lam_params = jnp.concatenate([lambda_q1, lambda_k1, lambda_q2, lambda_k2], axis=0).astype(F32)
    args = (p, ffn1_norm[0], mix_norm[0], lam_params, subln_gain[0], ffn2_norm[0], final_norm,
            head_dim, rnn_width)
    return _trunk(x_prompt, *args), _trunk(x_sample, *args)
```
